```python
import math
import jax, jax.numpy as jnp
from jax import lax
import numpy as np

D_MODEL = 2048
BATCH = 4
SEQ = 4096
DEPTH = 2

HEAD_DIM = 128
SB_HEADS = 8
DIFF_HEADS = 4
DIFF_V_DIM = 2 * HEAD_DIM
SB_WIDTH = SB_HEADS * HEAD_DIM
DIFF_QK_WIDTH = DIFF_HEADS * 2 * HEAD_DIM
DIFF_V_WIDTH = DIFF_HEADS * DIFF_V_DIM
N_BRANCHES = 2
IN_WIDTH = 3 * SB_WIDTH + 2 * DIFF_QK_WIDTH + DIFF_V_WIDTH + N_BRANCHES * D_MODEL
D_FF = 5632
PLE_DIM = 256
ROPE_THETA = 500000.0
ROT_DIM = HEAD_DIM // 4
Q_BLOCK = 128
EPS = 1e-6
FFN_RES_WEIGHT = 0.5

_OFF_SB_K = SB_WIDTH
_OFF_SB_V = 2 * SB_WIDTH
_OFF_DF_Q = 3 * SB_WIDTH
_OFF_DF_K = _OFF_DF_Q + DIFF_QK_WIDTH
_OFF_DF_V = _OFF_DF_K + DIFF_QK_WIDTH
_OFF_G_A = _OFF_DF_V + DIFF_V_WIDTH
_OFF_G_B = _OFF_G_A + D_MODEL

kernel_name = "hybrid_stickbreak_diffattn_macaron_ple"


def rmsnorm(x, g):
    xf = x.astype(jnp.float32)
    ms = jnp.mean(xf * xf, axis=-1, keepdims=True)
    return (xf * lax.rsqrt(ms + EPS) * g.astype(jnp.float32)).astype(x.dtype)


def swiglu_ffn(x, w_gu, w_down):
    gate, up = jnp.split(x @ w_gu, 2, axis=-1)
    return (jax.nn.silu(gate) * up) @ w_down


def rope_tables(seq):
    pos = jnp.arange(seq, dtype=jnp.float32)
    inv_freq = ROPE_THETA ** (-jnp.arange(0, ROT_DIM, 2, dtype=jnp.float32) / ROT_DIM)
    ang = pos[:, None] * inv_freq[None, :]
    return jnp.cos(ang), jnp.sin(ang)


def partial_rope(x, cos, sin):
    x_rot, x_pass = x[..., :ROT_DIM], x[..., ROT_DIM:]
    x1, x2 = jnp.split(x_rot, 2, axis=-1)
    c = cos[None, :, None, :].astype(x.dtype)
    s = sin[None, :, None, :].astype(x.dtype)
    return jnp.concatenate([x1 * c - x2 * s, x2 * c + x1 * s, x_pass], axis=-1)


def stick_breaking_attention(q, k, v):
    seq = q.shape[2]
    scale = HEAD_DIM ** -0.5
    outs = []
    for qb in range(seq // Q_BLOCK):
        q0 = qb * Q_BLOCK
        klen = q0 + Q_BLOCK
        z = jnp.einsum('bhqd,bhkd->bhqk', q[:, :, q0:klen], k[:, :, :klen]).astype(jnp.float32) * scale
        t_idx = q0 + jnp.arange(Q_BLOCK)[:, None]
        s_idx = jnp.arange(klen)[None, :]
        mask = s_idx < t_idx
        log_beta = jax.nn.log_sigmoid(z)
        log_keep = jnp.where(mask, jax.nn.log_sigmoid(-z), 0.0)
        later = lax.cumsum(log_keep, axis=3, reverse=True) - log_keep
        attn = jnp.where(mask, jnp.exp(log_beta + later), 0.0)
        outs.append(jnp.einsum('bhqk,bhkd->bhqd', attn.astype(v.dtype), v[:, :, :klen]))
    return jnp.concatenate(outs, axis=2)


def differential_attention(q1, q2, k1, k2, v, lam):
    seq = q1.shape[2]
    scale = HEAD_DIM ** -0.5
    outs = []
    for qb in range(seq // Q_BLOCK):
        q0 = qb * Q_BLOCK
        klen = q0 + Q_BLOCK
        mask = jnp.arange(klen)[None, :] <= (q0 + jnp.arange(Q_BLOCK))[:, None]
        s1 = jnp.einsum('bhqd,bhkd->bhqk', q1[:, :, q0:klen], k1[:, :, :klen]).astype(jnp.float32) * scale
        s2 = jnp.einsum('bhqd,bhkd->bhqk', q2[:, :, q0:klen], k2[:, :, :klen]).astype(jnp.float32) * scale
        a1 = jax.nn.softmax(jnp.where(mask, s1, -jnp.inf), axis=-1)
        a2 = jax.nn.softmax(jnp.where(mask, s2, -jnp.inf), axis=-1)
        attn = a1 - lam * a2
        outs.append(jnp.einsum('bhqk,bhkd->bhqd', attn.astype(v.dtype), v[:, :, :klen]))
    return jnp.concatenate(outs, axis=2)


def setup_inputs(seed: int = 0) -> dict:
    key = jax.random.key(seed)
    ks = iter(jax.random.split(key, 32))

    def w(shape, fan_in):
        return jax.random.normal(next(ks), shape, jnp.float32) * (fan_in ** -0.5)

    def gain(shape):
        return 1.0 + 0.02 * jax.random.normal(next(ks), shape, jnp.float32)

    def small(shape, s):
        return s * jax.random.normal(next(ks), shape, jnp.float32)

    return {
        "x": jax.random.normal(next(ks), (BATCH, SEQ, D_MODEL), jnp.float32),
        "p": jax.random.normal(next(ks), (DEPTH, BATCH, SEQ, PLE_DIM), jnp.float32),
        "ffn1_norm": gain((DEPTH, D_MODEL)),
        "ffn1_w_gu": w((DEPTH, D_MODEL, 2 * D_FF), D_MODEL),
        "ffn1_w_down": w((DEPTH, D_FF, D_MODEL), D_FF),
        "mix_norm": gain((DEPTH, D_MODEL)),
        "w_in": w((DEPTH, D_MODEL, IN_WIDTH), D_MODEL),
        "diff_q_norm": gain((DEPTH, HEAD_DIM)),
        "diff_k_norm": gain((DEPTH, HEAD_DIM)),
        "diff_lambda_q1": small((DEPTH, HEAD_DIM), 0.1),
        "diff_lambda_k1": small((DEPTH, HEAD_DIM), 0.1),
        "diff_lambda_q2": small((DEPTH, HEAD_DIM), 0.1),
        "diff_lambda_k2": small((DEPTH, HEAD_DIM), 0.1),
        "diff_sub_norm": gain((DEPTH, DIFF_V_DIM)),
        "w_branch_a": w((DEPTH, SB_WIDTH, D_MODEL), SB_WIDTH),
        "w_branch_b": w((DEPTH, DIFF_V_WIDTH, D_MODEL), DIFF_V_WIDTH),
        "w_out": w((DEPTH, D_MODEL, D_MODEL), D_MODEL),
        "ffn2_norm": gain((DEPTH, D_MODEL)),
        "ffn2_w_gu": w((DEPTH, D_MODEL, 2 * D_FF), D_MODEL),
        "ffn2_w_down": w((DEPTH, D_FF, D_MODEL), D_FF),
        "ple_norm": gain((DEPTH, D_MODEL)),
        "ple_w_gate": w((DEPTH, D_MODEL, D_MODEL), D_MODEL),
        "ple_w_proj": w((DEPTH, PLE_DIM, D_MODEL), PLE_DIM),
        "ple_out_norm": gain((DEPTH, D_MODEL)),
    }


def reference(x, p, ffn1_norm, ffn1_w_gu, ffn1_w_down, mix_norm, w_in,
              diff_q_norm, diff_k_norm, diff_lambda_q1, diff_lambda_k1,
              diff_lambda_q2, diff_lambda_k2, diff_sub_norm, w_branch_a,
              w_branch_b, w_out, ffn2_norm, ffn2_w_gu, ffn2_w_down, ple_norm,
              ple_w_gate, ple_w_proj, ple_out_norm):
    b, s, _ = x.shape
    cos, sin = rope_tables(s)
    h = x
    for i in range(DEPTH):
        lambda_init = 0.8 - 0.6 * math.exp(-0.3 * i)

        h = h + FFN_RES_WEIGHT * swiglu_ffn(rmsnorm(h, ffn1_norm[i]), ffn1_w_gu[i], ffn1_w_down[i])

        u = rmsnorm(h, mix_norm[i])
        proj = u @ w_in[i]
        sb_q = proj[..., :_OFF_SB_K].reshape(b, s, SB_HEADS, HEAD_DIM)
        sb_k = proj[..., _OFF_SB_K:_OFF_SB_V].reshape(b, s, SB_HEADS, HEAD_DIM)
        sb_v = proj[..., _OFF_SB_V:_OFF_DF_Q].reshape(b, s, SB_HEADS, HEAD_DIM)
        df_q = proj[..., _OFF_DF_Q:_OFF_DF_K].reshape(b, s, DIFF_HEADS, 2, HEAD_DIM)
        df_k = proj[..., _OFF_DF_K:_OFF_DF_V].reshape(b, s, DIFF_HEADS, 2, HEAD_DIM)
        df_v = proj[..., _OFF_DF_V:_OFF_G_A].reshape(b, s, DIFF_HEADS, DIFF_V_DIM)
        gate_a = jax.nn.sigmoid(proj[..., _OFF_G_A:_OFF_G_B])
        gate_b = jax.nn.sigmoid(proj[..., _OFF_G_B:])

        y_a = stick_breaking_attention(sb_q.transpose(0, 2, 1, 3), sb_k.transpose(0, 2, 1, 3),
                                       sb_v.transpose(0, 2, 1, 3))
        y_a = y_a.transpose(0, 2, 1, 3).reshape(b, s, SB_WIDTH)

        def prep(t, g):
            return partial_rope(rmsnorm(t, g), cos, sin).transpose(0, 2, 1, 3)
        q1 = prep(df_q[..., 0, :], diff_q_norm[i])
        q2 = prep(df_q[..., 1, :], diff_q_norm[i])
        k1 = prep(df_k[..., 0, :], diff_k_norm[i])
        k2 = prep(df_k[..., 1, :], diff_k_norm[i])
        lam = (jnp.exp(jnp.sum(diff_lambda_q1[i].astype(jnp.float32) * diff_lambda_k1[i].astype(jnp.float32)))
               - jnp.exp(jnp.sum(diff_lambda_q2[i].astype(jnp.float32) * diff_lambda_k2[i].astype(jnp.float32)))
               + lambda_init)
        y_b = differential_attention(q1, q2, k1, k2, df_v.transpose(0, 2, 1, 3), lam)
        y_b = rmsnorm(y_b.transpose(0, 2, 1, 3), diff_sub_norm[i]) * (1.0 - lambda_init)
        y_b = y_b.reshape(b, s, DIFF_V_WIDTH)

        merged = gate_a * (y_a @ w_branch_a[i]) + gate_b * (y_b @ w_branch_b[i])
        h = h + merged @ w_out[i]

        h = h + FFN_RES_WEIGHT * swiglu_ffn(rmsnorm(h, ffn2_norm[i]), ffn2_w_gu[i], ffn2_w_down[i])

        ple_gate = jax.nn.sigmoid(rmsnorm(h, ple_norm[i]) @ ple_w_gate[i])
        ple = rmsnorm(p[i] @ ple_w_proj[i], ple_out_norm[i])
        h = h + ple_gate * ple
    return h
```

```python
import functools
import math

import jax
import jax.numpy as jnp
from jax import lax
from jax.experimental import pallas as pl
from jax.experimental.pallas import tpu as pltpu

D_MODEL = 2048
DEPTH = 2
HEAD_DIM = 128
SB_HEADS = 8
DIFF_HEADS = 4
DIFF_V_DIM = 2 * HEAD_DIM
SB_WIDTH = SB_HEADS * HEAD_DIM
DIFF_QK_WIDTH = DIFF_HEADS * 2 * HEAD_DIM
DIFF_V_WIDTH = DIFF_HEADS * DIFF_V_DIM
D_FF = 5632
PLE_DIM = 256
ROPE_THETA = 500000.0
ROT_DIM = HEAD_DIM // 4
EPS = 1e-6
FFN_RES_WEIGHT = 0.5
QK_SCALE = HEAD_DIM ** -0.5

OFF_DF_Q = 3 * SB_WIDTH
OFF_DF_V = OFF_DF_Q + 2 * DIFF_QK_WIDTH
OFF_G_A = OFF_DF_V + DIFF_V_WIDTH
OFF_G_B = OFF_G_A + D_MODEL

F32 = jnp.float32
BF16 = jnp.bfloat16

VMEM_LIMIT_BYTES = 56 * 1024 * 1024
MXU_DIM = 256
MASK_VALUE = -1e30


def _params(*sem):
    return pltpu.CompilerParams(dimension_semantics=sem, vmem_limit_bytes=VMEM_LIMIT_BYTES)


def _rms(x, g):
    ms = jnp.mean(x * x, axis=-1, keepdims=True)
    return x * lax.rsqrt(ms + EPS) * g


def _dot(a, b):
    return jnp.dot(a, b, preferred_element_type=F32)


def _dot_nt(a, b):
    return lax.dot_general(a, b, (((1,), (1,)), ((), ())), preferred_element_type=F32)


def _ffn_kernel(h_ref, g_ref, wg_ref, wu_ref, wd_ref, o_ref, u_ref, *, tn):
    @pl.when(pl.program_id(1) == 0)
    def _():
        h = h_ref[...]
        u_ref[...] = _rms(h, g_ref[...]).astype(BF16)
        o_ref[...] = h

    u = u_ref[...]
    gate = _dot(u, wg_ref[...])
    up = _dot(u, wu_ref[...])
    a = (gate * jax.nn.sigmoid(gate) * up).astype(BF16)
    for n in range(D_MODEL // tn):
        cols = slice(n * tn, (n + 1) * tn)
        o_ref[:, cols] += FFN_RES_WEIGHT * _dot(a, wd_ref[:, cols])


def _ffn(h, gain, w_gu, w_down, layer, *, tm=512, tf=512, tn=512):
    t = h.shape[0]
    nf = D_FF // tf
    return pl.pallas_call(
        functools.partial(_ffn_kernel, tn=tn),
        grid=(t // tm, nf),
        in_specs=[
            pl.BlockSpec((tm, D_MODEL), lambda i, k: (i, 0)),
            pl.BlockSpec((None, 1, D_MODEL), lambda i, k: (layer, 0, 0)),
            pl.BlockSpec((None, D_MODEL, tf), lambda i, k: (layer, 0, k)),
            pl.BlockSpec((None, D_MODEL, tf), lambda i, k: (layer, 0, k + nf)),
            pl.BlockSpec((None, tf, D_MODEL), lambda i, k: (layer, k, 0)),
        ],
        out_specs=pl.BlockSpec((tm, D_MODEL), lambda i, k: (i, 0)),
        out_shape=jax.ShapeDtypeStruct((t, D_MODEL), F32),
        scratch_shapes=[pltpu.VMEM((tm, D_MODEL), BF16)],
        compiler_params=_params("parallel", "arbitrary"),
        name="ffn",
    )(h, gain, w_gu, w_gu, w_down)


def _proj_heads_kernel(h_ref, g_ref, w_ref, o_ref, u_ref, *, hw, scale_block0):
    j = pl.program_id(1)

    @pl.when(j == 0)
    def _():
        u_ref[...] = _rms(h_ref[...], g_ref[...]).astype(BF16)

    r = _dot(u_ref[...], w_ref[...])
    if scale_block0:
        r = r * jnp.where(j == 0, QK_SCALE, 1.0).astype(F32)
    for hh in range(r.shape[1] // hw):
        o_ref[hh] = r[:, hh * hw:(hh + 1) * hw].astype(BF16)


def _proj_heads(h, gain, w_in, layer, col0, ncols, hw, scale_block0, *, tm=1024, tn=1024):
    t = h.shape[0]
    cb0 = col0 // tn
    hpb = tn // hw
    return pl.pallas_call(
        functools.partial(_proj_heads_kernel, hw=hw, scale_block0=scale_block0),
        grid=(t // tm, ncols // tn),
        in_specs=[
            pl.BlockSpec((tm, D_MODEL), lambda i, j: (i, 0)),
            pl.BlockSpec((None, 1, D_MODEL), lambda i, j: (layer, 0, 0)),
            pl.BlockSpec((None, D_MODEL, tn), lambda i, j: (layer, 0, cb0 + j)),
        ],
        out_specs=pl.BlockSpec((hpb, tm, hw), lambda i, j: (j, i, 0)),
        out_shape=jax.ShapeDtypeStruct((ncols // hw, t, hw), BF16),
        scratch_shapes=[pltpu.VMEM((tm, D_MODEL), BF16)],
        compiler_params=_params("parallel", "arbitrary"),
        name="proj_heads",
    )(h, gain, w_in)


def _proj_qk_kernel(h_ref, g_ref, w_ref, qkg_ref, cos_ref, sin_ref, o_ref, u_ref):
    j = pl.program_id(1)

    @pl.when(j == 0)
    def _():
        u_ref[...] = _rms(h_ref[...], g_ref[...]).astype(BF16)

    r = _dot(u_ref[...], w_ref[...])
    gain = qkg_ref[...] * jnp.where(j == 0, QK_SCALE, 1.0).astype(F32)
    cos = cos_ref[...]
    sin = sin_ref[...]
    lane = lax.broadcasted_iota(jnp.int32, (r.shape[0], HEAD_DIM), 1)
    first_half = lane < ROT_DIM // 2
    for c in range(r.shape[1] // HEAD_DIM):
        x = _rms(r[:, c * HEAD_DIM:(c + 1) * HEAD_DIM], gain)
        partner = jnp.where(first_half,
                            pltpu.roll(x, HEAD_DIM - ROT_DIM // 2, 1),
                            pltpu.roll(x, ROT_DIM // 2, 1))
        o_ref[c] = (x * cos + partner * sin).astype(BF16)


def _proj_qk(h, gain, w_in, qk_gain, cos_t, sin_t, layer, seq, *, tm=1024, tn=1024):
    t = h.shape[0]
    cb0 = OFF_DF_Q // tn
    ncols = 2 * DIFF_QK_WIDTH
    hpb = tn // HEAD_DIM
    nseq = seq // tm
    return pl.pallas_call(
        _proj_qk_kernel,
        grid=(t // tm, ncols // tn),
        in_specs=[
            pl.BlockSpec((tm, D_MODEL), lambda i, j: (i, 0)),
            pl.BlockSpec((None, 1, D_MODEL), lambda i, j: (layer, 0, 0)),
            pl.BlockSpec((None, D_MODEL, tn), lambda i, j: (layer, 0, cb0 + j)),
            pl.BlockSpec((None, None, 1, HEAD_DIM), lambda i, j: (j, layer, 0, 0)),
            pl.BlockSpec((tm, HEAD_DIM), lambda i, j: (i % nseq, 0)),
            pl.BlockSpec((tm, HEAD_DIM), lambda i, j: (i % nseq, 0)),
        ],
        out_specs=pl.BlockSpec((hpb, tm, HEAD_DIM), lambda i, j: (j, i, 0)),
        out_shape=jax.ShapeDtypeStruct((ncols // HEAD_DIM, t, HEAD_DIM), BF16),
        scratch_shapes=[pltpu.VMEM((tm, D_MODEL), BF16)],
        compiler_params=_params("parallel", "arbitrary"),
        name="proj_qk",
    )(h, gain, w_in, qk_gain, cos_t, sin_t)


def _sb_kernel(q_ref, k_ref, v_ref, o_ref, *, tq, tk):
    qi = pl.program_id(2)
    q = q_ref[...]
    r_i = lax.broadcasted_iota(jnp.int32, (MXU_DIM, MXU_DIM), 0)
    c_i = lax.broadcasted_iota(jnp.int32, (MXU_DIM, MXU_DIM), 1)
    tri = jnp.where(r_i > c_i, 1.0, 0.0).astype(BF16)
    tri2 = jnp.concatenate([tri, tri], axis=0)
    nsub = tk // MXU_DIM

    def block(kb, carry, acc, masked):
        start = pl.multiple_of(kb * tk, tk)
        ks = k_ref[pl.ds(start, tk), :]
        vs = v_ref[pl.ds(start, tk), :]
        z = _dot_nt(q, ks)
        sp = jnp.maximum(z, 0.0) + jnp.log(1.0 + jnp.exp(-jnp.abs(z)))
        log_keep = -sp
        log_beta = z - sp
        if masked:
            t_i = lax.broadcasted_iota(jnp.int32, (tq, tk), 0)
            s_i = lax.broadcasted_iota(jnp.int32, (tq, tk), 1)
            mask = s_i < t_i
            log_keep = jnp.where(mask, log_keep, 0.0)
        parts = [None] * nsub
        for c in reversed(range(nsub)):
            cols = slice(c * MXU_DIM, (c + 1) * MXU_DIM)
            lk = log_keep[:, cols]
            hi = lk.astype(BF16)
            lo = (lk - hi.astype(F32)).astype(BF16)
            later = _dot(jnp.concatenate([hi, lo], axis=1), tri2) + carry
            carry = carry + jnp.sum(lk, axis=1, keepdims=True)
            a = jnp.exp(log_beta[:, cols] + later)
            if masked:
                a = jnp.where(mask[:, cols], a, 0.0)
            parts[c] = a.astype(BF16)
        attn = jnp.concatenate(parts, axis=1)
        return carry, acc + _dot(attn, vs)

    carry0 = jnp.zeros((tq, 1), F32)
    acc0 = jnp.zeros((tq, HEAD_DIM), F32)
    carry, acc = block(qi, carry0, acc0, True)

    def body(it, state):
        return block(qi - 1 - it, state[0], state[1], False)

    _, acc = lax.fori_loop(0, qi, body, (carry, acc))
    o_ref[...] = acc.astype(BF16)


def _sb_attention(qkv, batch, seq, *, tq=512):
    t = batch * seq
    nq = seq // tq
    return pl.pallas_call(
        functools.partial(_sb_kernel, tq=tq, tk=tq),
        grid=(batch, SB_HEADS, nq),
        in_specs=[
            pl.BlockSpec((None, tq, HEAD_DIM), lambda b, h, i: (h, b * nq + i, 0)),
            pl.BlockSpec((None, seq, HEAD_DIM), lambda b, h, i: (SB_HEADS + h, b, 0)),
            pl.BlockSpec((None, seq, HEAD_DIM), lambda b, h, i: (2 * SB_HEADS + h, b, 0)),
        ],
        out_specs=pl.BlockSpec((tq, HEAD_DIM), lambda b, h, i: (b * nq + i, h)),
        out_shape=jax.ShapeDtypeStruct((t, SB_WIDTH), BF16),
        compiler_params=_params("parallel", "parallel", "arbitrary"),
        name="sb_attention",
    )(qkv, qkv, qkv)


def _diff_kernel(lq1_ref, lk1_ref, lq2_ref, lk2_ref, sub_ref, q_ref, k_ref, v_ref, o_ref,
                 *, tq, tk, lambda_init):
    qi = pl.program_id(2)
    qs = (q_ref[0], q_ref[1])

    def block(kb, state, masked):
        start = pl.multiple_of(kb * tk, tk)
        vs = v_ref[pl.ds(start, tk), :]
        if masked:
            t_i = lax.broadcasted_iota(jnp.int32, (tq, tk), 0)
            s_i = lax.broadcasted_iota(jnp.int32, (tq, tk), 1)
            mask = s_i <= t_i
        new_state = []
        for half in range(2):
            m, l, acc = state[half]
            s = _dot_nt(qs[half], k_ref[half, pl.ds(start, tk), :])
            if masked:
                s = jnp.where(mask, s, MASK_VALUE)
            m_new = jnp.maximum(m, jnp.max(s, axis=1, keepdims=True))
            alpha = jnp.exp(m - m_new)
            p = jnp.exp(s - m_new)
            l = alpha * l + jnp.sum(p, axis=1, keepdims=True)
            acc = alpha * acc + _dot(p.astype(BF16), vs)
            new_state.append((m_new, l, acc))
        return tuple(new_state)

    def init():
        return (jnp.full((tq, 1), MASK_VALUE, F32), jnp.zeros((tq, 1), F32),
                jnp.zeros((tq, DIFF_V_DIM), F32))

    state = block(qi, (init(), init()), True)
    state = lax.fori_loop(0, qi, lambda it, st: block(qi - 1 - it, st, False), state)

    lam = (jnp.exp(jnp.sum(lq1_ref[...] * lk1_ref[...], axis=1, keepdims=True))
           - jnp.exp(jnp.sum(lq2_ref[...] * lk2_ref[...], axis=1, keepdims=True))
           + lambda_init)
    (_, l1, acc1), (_, l2, acc2) = state
    y = acc1 / l1 - lam * (acc2 / l2)
    o_ref[...] = (_rms(y, sub_ref[...]) * (1.0 - lambda_init)).astype(BF16)


def _diff_attention(lams, sub_gain, qk, v, layer, batch, seq, lambda_init, *, tq=512):
    t = batch * seq
    nq = seq // tq
    lam_spec = pl.BlockSpec((None, 1, HEAD_DIM), lambda b, h, i: (layer, 0, 0))
    return pl.pallas_call(
        functools.partial(_diff_kernel, tq=tq, tk=tq, lambda_init=lambda_init),
        grid=(batch, DIFF_HEADS, nq),
        in_specs=[
            lam_spec, lam_spec, lam_spec, lam_spec,
            pl.BlockSpec((None, 1, DIFF_V_DIM), lambda b, h, i: (layer, 0, 0)),
            pl.BlockSpec((2, tq, HEAD_DIM), lambda b, h, i: (h, b * nq + i, 0)),
            pl.BlockSpec((2, seq, HEAD_DIM), lambda b, h, i: (DIFF_HEADS + h, b, 0)),
            pl.BlockSpec((None, seq, DIFF_V_DIM), lambda b, h, i: (h, b, 0)),
        ],
        out_specs=pl.BlockSpec((tq, DIFF_V_DIM), lambda b, h, i: (b * nq + i, h)),
        out_shape=jax.ShapeDtypeStruct((t, DIFF_V_WIDTH), BF16),
        compiler_params=_params("parallel", "parallel", "arbitrary"),
        name="diff_attention",
    )(*lams, sub_gain, qk, qk, v)


def _merge_kernel(h_ref, g_ref, wga_ref, wgb_ref, ya_ref, yb_ref, wa_ref, wb_ref, o_ref, u_ref):
    @pl.when(pl.program_id(1) == 0)
    def _():
        u_ref[...] = _rms(h_ref[...], g_ref[...]).astype(BF16)

    u = u_ref[...]
    merged = (jax.nn.sigmoid(_dot(u, wga_ref[...])) * _dot(ya_ref[...], wa_ref[...])
              + jax.nn.sigmoid(_dot(u, wgb_ref[...])) * _dot(yb_ref[...], wb_ref[...]))
    o_ref[...] = merged.astype(BF16)


def _merge(h, gain, w_in, ya, yb, wa, wb, layer, *, tm=512, tn=512):
    t = h.shape[0]
    ga0 = OFF_G_A // tn
    gb0 = OFF_G_B // tn
    return pl.pallas_call(
        _merge_kernel,
        grid=(t // tm, D_MODEL // tn),
        in_specs=[
            pl.BlockSpec((tm, D_MODEL), lambda i, j: (i, 0)),
            pl.BlockSpec((None, 1, D_MODEL), lambda i, j: (layer, 0, 0)),
            pl.BlockSpec((None, D_MODEL, tn), lambda i, j: (layer, 0, ga0 + j)),
            pl.BlockSpec((None, D_MODEL, tn), lambda i, j: (layer, 0, gb0 + j)),
            pl.BlockSpec((tm, SB_WIDTH), lambda i, j: (i, 0)),
            pl.BlockSpec((tm, DIFF_V_WIDTH), lambda i, j: (i, 0)),
            pl.BlockSpec((None, SB_WIDTH, tn), lambda i, j: (layer, 0, j)),
            pl.BlockSpec((None, DIFF_V_WIDTH, tn), lambda i, j: (layer, 0, j)),
        ],
        out_specs=pl.BlockSpec((tm, tn), lambda i, j: (i, j)),
        out_shape=jax.ShapeDtypeStruct((t, D_MODEL), BF16),
        scratch_shapes=[pltpu.VMEM((tm, D_MODEL), BF16)],
        compiler_params=_params("parallel", "arbitrary"),
        name="merge",
    )(h, gain, w_in, w_in, ya, yb, wa, wb)


def _out_kernel(h_ref, m_ref, w_ref, o_ref):
    o_ref[...] = h_ref[...] + _dot(m_ref[...], w_ref[...])


def _out_proj(h, merged, w_out, layer, *, tm=1024, tn=1024):
    t = h.shape[0]
    return pl.pallas_call(
        _out_kernel,
        grid=(t // tm, D_MODEL // tn),
        in_specs=[
            pl.BlockSpec((tm, tn), lambda i, j: (i, j)),
            pl.BlockSpec((tm, D_MODEL), lambda i, j: (i, 0)),
            pl.BlockSpec((None, D_MODEL, tn), lambda i, j: (layer, 0, j)),
        ],
        out_specs=pl.BlockSpec((tm, tn), lambda i, j: (i, j)),
        out_shape=jax.ShapeDtypeStruct((t, D_MODEL), F32),
        compiler_params=_params("parallel", "arbitrary"),
        name="out_proj",
    )(h, merged, w_out)


def _ple_kernel(h_ref, g_ref, wg_ref, p_ref, wp_ref, og_ref, o_ref):
    h = h_ref[...]
    u = _rms(h, g_ref[...]).astype(BF16)
    gate = jax.nn.sigmoid(_dot(u, wg_ref[...]))
    emb = _rms(_dot(p_ref[...].astype(BF16), wp_ref[...]), og_ref[...])
    o_ref[...] = h + gate * emb


def _ple(h, gain, w_gate, p, w_proj, out_gain, layer, *, tm=512):
    t = h.shape[0]
    nt = t // tm
    return pl.pallas_call(
        _ple_kernel,
        grid=(nt,),
        in_specs=[
            pl.BlockSpec((tm, D_MODEL), lambda i: (i, 0)),
            pl.BlockSpec((None, 1, D_MODEL), lambda i: (layer, 0, 0)),
            pl.BlockSpec((None, D_MODEL, D_MODEL), lambda i: (layer, 0, 0)),
            pl.BlockSpec((None, tm, PLE_DIM), lambda i: (layer, i, 0)),
            pl.BlockSpec((None, PLE_DIM, D_MODEL), lambda i: (layer, 0, 0)),
            pl.BlockSpec((None, 1, D_MODEL), lambda i: (layer, 0, 0)),
        ],
        out_specs=pl.BlockSpec((tm, D_MODEL), lambda i: (i, 0)),
        out_shape=jax.ShapeDtypeStruct((t, D_MODEL), F32),
        compiler_params=_params("parallel"),
        name="ple",
    )(h, gain, w_gate, p, w_proj, out_gain)


def _rope_tables(seq):
    pos = jnp.arange(seq, dtype=F32)
    inv_freq = ROPE_THETA ** (-jnp.arange(0, ROT_DIM, 2, dtype=F32) / ROT_DIM)
    ang = pos[:, None] * inv_freq[None, :]
    cos, sin = jnp.cos(ang), jnp.sin(ang)
    pad = HEAD_DIM - ROT_DIM
    cos_t = jnp.concatenate([cos, cos, jnp.ones((seq, pad), F32)], axis=1)
    sin_t = jnp.concatenate([-sin, sin, jnp.zeros((seq, pad), F32)], axis=1)
    return cos_t, sin_t


def kernel(x, p, ffn1_norm, ffn1_w_gu, ffn1_w_down, mix_norm, w_in, diff_q_norm, diff_k_norm,
           diff_lambda_q1, diff_lambda_k1, diff_lambda_q2, diff_lambda_k2, diff_sub_norm,
           w_branch_a, w_branch_b, w_out, ffn2_norm, ffn2_w_gu, ffn2_w_down, ple_norm,
           ple_w_gate, ple_w_proj, ple_out_norm):
    batch, seq, _ = x.shape
    t = batch * seq
    row = lambda a: a.reshape(DEPTH, 1, a.shape[-1])
    bf = lambda a: a.astype(BF16)

    ffn1_w_gu, ffn1_w_down, ffn2_w_gu, ffn2_w_down = map(
        bf, (ffn1_w_gu, ffn1_w_down, ffn2_w_gu, ffn2_w_down))
    w_in, w_branch_a, w_branch_b, w_out = map(bf, (w_in, w_branch_a, w_branch_b, w_out))
    ple_w_gate, ple_w_proj = bf(ple_w_gate), bf(ple_w_proj)
    ffn1_norm, mix_norm, ffn2_norm, ple_norm, ple_out_norm, diff_sub_norm = map(
        row, (ffn1_norm, mix_norm, ffn2_norm, ple_norm, ple_out_norm, diff_sub_norm))
    lams = tuple(map(row, (diff_lambda_q1, diff_lambda_k1, diff_lambda_q2, diff_lambda_k2)))
    qk_gain = jnp.stack([row(diff_q_norm), row(diff_k_norm)])
    cos_t, sin_t = _rope_tables(seq)
    p = p.reshape(DEPTH, t, PLE_DIM)

    h = x.reshape(t, D_MODEL)
    for layer in range(DEPTH):
        lambda_init = 0.8 - 0.6 * math.exp(-0.3 * layer)
        h = _ffn(h, ffn1_norm, ffn1_w_gu, ffn1_w_down, layer)

        sb_qkv = _proj_heads(h, mix_norm, w_in, layer, 0, 3 * SB_WIDTH, HEAD_DIM, True)
        df_qk = _proj_qk(h, mix_norm, w_in, qk_gain, cos_t, sin_t, layer, seq)
        df_v = _proj_heads(h, mix_norm, w_in, layer, OFF_DF_V, DIFF_V_WIDTH, DIFF_V_DIM, False)
        y_a = _sb_attention(sb_qkv, batch, seq)
        y_b = _diff_attention(lams, diff_sub_norm, df_qk, df_v, layer, batch, seq, lambda_init)
        merged = _merge(h, mix_norm, w_in, y_a, y_b, w_branch_a, w_branch_b, layer)
        h = _out_proj(h, merged, w_out, layer)

        h = _ffn(h, ffn2_norm, ffn2_w_gu, ffn2_w_down, layer)
        h = _ple(h, ple_norm, ple_w_gate, p, ple_w_proj, ple_out_norm, layer)
    return h.reshape(batch, seq, D_MODEL)
```

```python
import functools
import math

import jax
import jax.numpy as jnp
from jax import lax
from jax.experimental import pallas as pl
from jax.experimental.pallas import tpu as pltpu

D_MODEL = 2048
DEPTH = 2
HEAD_DIM = 128
SB_HEADS = 8
DIFF_HEADS = 4
DIFF_V_DIM = 2 * HEAD_DIM
SB_WIDTH = SB_HEADS * HEAD_DIM
DIFF_QK_WIDTH = DIFF_HEADS * 2 * HEAD_DIM
DIFF_V_WIDTH = DIFF_HEADS * DIFF_V_DIM
D_FF = 5632
PLE_DIM = 256
ROPE_THETA = 500000.0
ROT_DIM = HEAD_DIM // 4
EPS = 1e-6
FFN_RES_WEIGHT = 0.5
QK_SCALE = HEAD_DIM ** -0.5 * math.log2(math.e)

OFF_DF_Q = 3 * SB_WIDTH
OFF_DF_V = OFF_DF_Q + 2 * DIFF_QK_WIDTH
OFF_G_A = OFF_DF_V + DIFF_V_WIDTH
OFF_G_B = OFF_G_A + D_MODEL

F32 = jnp.float32
BF16 = jnp.bfloat16

VMEM_LIMIT_BYTES = 56 * 1024 * 1024
SUBLANES = 8
MXU_DIM = 256
MASK_VALUE = -1e30


def _params(*sem):
    return pltpu.CompilerParams(dimension_semantics=sem, vmem_limit_bytes=VMEM_LIMIT_BYTES)


def _rms(x, g):
    ms = jnp.mean(x * x, axis=-1, keepdims=True)
    return x * lax.rsqrt(ms + EPS) * g


def _dot(a, b):
    return jnp.dot(a, b, preferred_element_type=F32)


def _dot_nt(a, b):
    return lax.dot_general(a, b, (((1,), (1,)), ((), ())), preferred_element_type=F32)


def _ffn_kernel(h_ref, g_ref, wg_ref, wu_ref, wd_ref, o_ref, u_ref, *, tn):
    @pl.when(pl.program_id(1) == 0)
    def _():
        h = h_ref[...]
        u_ref[...] = _rms(h, g_ref[...]).astype(BF16)
        o_ref[...] = h

    u = u_ref[...]
    gate = _dot(u, wg_ref[...])
    up = _dot(u, wu_ref[...])
    a = (gate * jax.nn.sigmoid(gate) * up).astype(BF16)
    for n in range(D_MODEL // tn):
        cols = slice(n * tn, (n + 1) * tn)
        o_ref[:, cols] += FFN_RES_WEIGHT * _dot(a, wd_ref[:, cols])


def _ffn(h, gain, w_gu, w_down, layer, *, tm=1024, tf=512, tn=512):
    t = h.shape[0]
    nf = D_FF // tf
    return pl.pallas_call(
        functools.partial(_ffn_kernel, tn=tn),
        grid=(t // tm, nf),
        in_specs=[
            pl.BlockSpec((tm, D_MODEL), lambda i, k: (i, 0), pipeline_mode=pl.Buffered(1)),
            pl.BlockSpec((None, 1, D_MODEL), lambda i, k: (layer, 0, 0)),
            pl.BlockSpec((None, D_MODEL, tf), lambda i, k: (layer, 0, k)),
            pl.BlockSpec((None, D_MODEL, tf), lambda i, k: (layer, 0, k + nf)),
            pl.BlockSpec((None, tf, D_MODEL), lambda i, k: (layer, k, 0)),
        ],
        out_specs=pl.BlockSpec((tm, D_MODEL), lambda i, k: (i, 0)),
        out_shape=jax.ShapeDtypeStruct((t, D_MODEL), F32),
        scratch_shapes=[pltpu.VMEM((tm, D_MODEL), BF16)],
        compiler_params=_params("parallel", "arbitrary"),
        name="ffn",
    )(h, gain, w_gu, w_gu, w_down)


def _proj_heads_kernel(h_ref, g_ref, w_ref, o_ref, u_ref, *, hw, block0_scale):
    j = pl.program_id(1)

    @pl.when(j == 0)
    def _():
        u_ref[...] = _rms(h_ref[...], g_ref[...]).astype(BF16)

    r = _dot(u_ref[...], w_ref[...])
    if block0_scale is not None:
        r = r * jnp.where(j == 0, block0_scale, 1.0).astype(F32)
    for hh in range(r.shape[1] // hw):
        o_ref[hh] = r[:, hh * hw:(hh + 1) * hw].astype(BF16)


def _proj_heads(h, gain, w_in, layer, col0, ncols, hw, block0_scale, *, tm=1024, tn=1024):
    t = h.shape[0]
    cb0 = col0 // tn
    hpb = tn // hw
    return pl.pallas_call(
        functools.partial(_proj_heads_kernel, hw=hw, block0_scale=block0_scale),
        grid=(t // tm, ncols // tn),
        in_specs=[
            pl.BlockSpec((tm, D_MODEL), lambda i, j: (i, 0)),
            pl.BlockSpec((None, 1, D_MODEL), lambda i, j: (layer, 0, 0)),
            pl.BlockSpec((None, D_MODEL, tn), lambda i, j: (layer, 0, cb0 + j)),
        ],
        out_specs=pl.BlockSpec((hpb, tm, hw), lambda i, j: (j, i, 0)),
        out_shape=jax.ShapeDtypeStruct((ncols // hw, t, hw), BF16),
        scratch_shapes=[pltpu.VMEM((tm, D_MODEL), BF16)],
        compiler_params=_params("parallel", "arbitrary"),
        name="proj_heads",
    )(h, gain, w_in)


def _proj_qk_kernel(h_ref, g_ref, w_ref, qkg_ref, cos_ref, sin_ref, o_ref, u_ref):
    j = pl.program_id(1)

    @pl.when(j == 0)
    def _():
        u_ref[...] = _rms(h_ref[...], g_ref[...]).astype(BF16)

    u = u_ref[...]
    gain = qkg_ref[...] * jnp.where(j == 0, QK_SCALE, 1.0).astype(F32)
    cos = cos_ref[...]
    sin = sin_ref[...]

    def finish(r, c0):
        for c in range(MXU_DIM // HEAD_DIM):
            x = _rms(r[:, c * HEAD_DIM:(c + 1) * HEAD_DIM], gain)
            partner = pltpu.roll(x, HEAD_DIM // 2, 1)
            o_ref[c0 // HEAD_DIM + c] = (x * cos + partner * sin).astype(BF16)

    starts = list(range(0, w_ref.shape[1], MXU_DIM))
    r = _dot(u, w_ref[:, starts[0]:starts[0] + MXU_DIM])
    for prev, c0 in zip(starts, starts[1:]):
        r_next = _dot(u, w_ref[:, c0:c0 + MXU_DIM])
        finish(r, prev)
        r = r_next
    finish(r, starts[-1])


def _proj_qk(h, gain, w_qk, qk_gain, cos_t, sin_t, layer, seq, *, tm=1024, tn=1024):
    t = h.shape[0]
    ncols = 2 * DIFF_QK_WIDTH
    hpb = tn // HEAD_DIM
    nseq = seq // tm
    return pl.pallas_call(
        _proj_qk_kernel,
        grid=(t // tm, ncols // tn),
        in_specs=[
            pl.BlockSpec((tm, D_MODEL), lambda i, j: (i, 0)),
            pl.BlockSpec((None, 1, D_MODEL), lambda i, j: (layer, 0, 0)),
            pl.BlockSpec((None, D_MODEL, tn), lambda i, j: (layer, 0, j)),
            pl.BlockSpec((None, None, 1, HEAD_DIM), lambda i, j: (j, layer, 0, 0)),
            pl.BlockSpec((tm, HEAD_DIM), lambda i, j: (i % nseq, 0)),
            pl.BlockSpec((tm, HEAD_DIM), lambda i, j: (i % nseq, 0)),
        ],
        out_specs=pl.BlockSpec((hpb, tm, HEAD_DIM), lambda i, j: (j, i, 0)),
        out_shape=jax.ShapeDtypeStruct((ncols // HEAD_DIM, t, HEAD_DIM), BF16),
        scratch_shapes=[pltpu.VMEM((tm, D_MODEL), BF16)],
        compiler_params=_params("parallel", "arbitrary"),
        name="proj_qk",
    )(h, gain, w_qk, qk_gain, cos_t, sin_t)


def _sb_kernel(q_ref, k_ref, v_ref, o_ref, z_scr, w_scr, acc_scr, *, tq, tk):
    qi = pl.program_id(2)
    q_t = q_ref[...].astype(F32).T.astype(BF16)
    ngroups = tk // SUBLANES
    seg = tk // SUBLANES
    group = lambda g: slice(g * SUBLANES, (g + 1) * SUBLANES)

    def logits(kb):
        z_scr[...] = _dot(k_ref[kb], q_t)

    def weights(masked):
        if masked:
            sub = lax.broadcasted_iota(jnp.int32, (SUBLANES, tq), 0)
            t_idx = lax.broadcasted_iota(jnp.int32, (SUBLANES, tq), 1)
        run = jnp.ones((SUBLANES, tq), F32)
        for g in range(ngroups):
            beta = 1.0 / (1.0 + jnp.exp2(z_scr[group(g), :]))
            keep = 1.0 - beta
            if masked:
                mask = sub * seg + (seg - 1 - g) < t_idx
                beta = jnp.where(mask, beta, 0.0)
                keep = jnp.where(mask, keep, 1.0)
            w_scr[group(g), :] = beta * run
            run = run * keep
        return run

    def accumulate(kb, totals, carry):
        offs = [None] * SUBLANES
        for i in reversed(range(SUBLANES)):
            offs[i] = carry
            carry = carry * totals[i:i + 1, :]
        off = jnp.concatenate(offs, axis=0)
        attn = jnp.concatenate([w_scr[group(g), :] * off for g in range(ngroups)],
                               axis=0).astype(BF16)
        acc_scr[...] += _dot(v_ref[kb], attn)
        return carry

    def body(it, state):
        totals, carry = state
        kb = qi - it
        carry = accumulate(kb, totals, carry)
        totals = weights(False)
        logits(jnp.maximum(kb - 2, 0))
        return totals, carry

    acc_scr[...] = jnp.zeros_like(acc_scr)
    logits(qi)
    totals = weights(True)
    logits(jnp.maximum(qi - 1, 0))
    totals, carry = lax.fori_loop(0, qi, body, (totals, jnp.ones((1, tq), F32)))
    accumulate(0, totals, carry)
    o_ref[...] = acc_scr[...].T.astype(BF16)


def _interleave_segments(a, tk):
    h, t, d = a.shape
    a = a.reshape(h, t // tk, SUBLANES, tk // SUBLANES, d)[:, :, :, ::-1, :]
    return a.transpose(0, 1, 3, 2, 4).reshape(h, t // tk, tk, d)


def _sb_attention(qkv, batch, seq, *, tq=512):
    t = batch * seq
    nq = seq // tq
    q = qkv[:SB_HEADS]
    k = _interleave_segments(qkv[SB_HEADS:2 * SB_HEADS], tq)
    v_t = _interleave_segments(qkv[2 * SB_HEADS:], tq).swapaxes(2, 3)
    return pl.pallas_call(
        functools.partial(_sb_kernel, tq=tq, tk=tq),
        grid=(batch, SB_HEADS, nq),
        in_specs=[
            pl.BlockSpec((None, tq, HEAD_DIM), lambda b, h, i: (h, b * nq + i, 0)),
            pl.BlockSpec((None, nq, tq, HEAD_DIM), lambda b, h, i: (h, b, 0, 0)),
            pl.BlockSpec((None, nq, HEAD_DIM, tq), lambda b, h, i: (h, b, 0, 0)),
        ],
        out_specs=pl.BlockSpec((tq, HEAD_DIM), lambda b, h, i: (b * nq + i, h)),
        out_shape=jax.ShapeDtypeStruct((t, SB_WIDTH), BF16),
        scratch_shapes=[pltpu.VMEM((tq, tq), F32), pltpu.VMEM((tq, tq), F32),
                        pltpu.VMEM((HEAD_DIM, tq), F32)],
        compiler_params=_params("parallel", "parallel", "arbitrary"),
        name="sb_attention",
    )(q, k, v_t)


def _diff_kernel(lq1_ref, lk1_ref, lq2_ref, lk2_ref, sub_ref, q_ref, k_ref, v_ref, o_ref,
                 s_scr, acc_scr, *, tq, tk, lambda_init):
    qi = pl.program_id(2)
    q_ts = [q_ref[half].astype(F32).T.astype(BF16) for half in range(2)]

    def logits(kb, masked):
        maxima = []
        for half in range(2):
            s = _dot(k_ref[half, kb], q_ts[half])
            if masked:
                s_idx = lax.broadcasted_iota(jnp.int32, (tk, tq), 0)
                t_idx = lax.broadcasted_iota(jnp.int32, (tk, tq), 1)
                s = jnp.where(s_idx <= t_idx, s, MASK_VALUE)
            s_scr[half] = s
            maxima.append(jnp.max(s, axis=0, keepdims=True))
        return tuple(maxima)

    def accumulate(kb, maxima, stats):
        v_t = v_ref[kb]
        new_stats = []
        for half in range(2):
            m, l = stats[half]
            m_new = jnp.maximum(m, maxima[half])
            alpha = jnp.exp2(m - m_new)
            p = jnp.exp2(s_scr[half] - m_new)
            l = alpha * l + jnp.sum(p, axis=0, keepdims=True)
            acc_scr[half] = alpha * acc_scr[half] + _dot(v_t, p.astype(BF16))
            new_stats.append((m_new, l))
        return tuple(new_stats)

    def body(it, state):
        maxima, stats = state
        kb = qi - it
        stats = accumulate(kb, maxima, stats)
        return logits(kb - 1, False), stats

    init = (jnp.full((1, tq), MASK_VALUE, F32), jnp.zeros((1, tq), F32))
    acc_scr[...] = jnp.zeros_like(acc_scr)
    maxima, stats = lax.fori_loop(0, qi, body, (logits(qi, True), (init, init)))
    (_, l1), (_, l2) = accumulate(0, maxima, stats)

    lam = (jnp.exp(jnp.sum(lq1_ref[...] * lk1_ref[...], axis=1, keepdims=True))
           - jnp.exp(jnp.sum(lq2_ref[...] * lk2_ref[...], axis=1, keepdims=True))
           + lambda_init)
    y = (acc_scr[0] / l1 - lam * (acc_scr[1] / l2)).T
    o_ref[...] = (_rms(y, sub_ref[...]) * (1.0 - lambda_init)).astype(BF16)


def _diff_attention(lams, sub_gain, qk, v, layer, batch, seq, lambda_init, *, tq=512):
    t = batch * seq
    nq = seq // tq
    nh = 2 * DIFF_HEADS
    q = qk[:nh]
    k = qk[nh:].reshape(nh, t // tq, tq, HEAD_DIM)
    v_t = v.reshape(DIFF_HEADS, t // tq, tq, DIFF_V_DIM).swapaxes(2, 3)
    lam_spec = pl.BlockSpec((None, 1, HEAD_DIM), lambda b, h, i: (layer, 0, 0))
    return pl.pallas_call(
        functools.partial(_diff_kernel, tq=tq, tk=tq, lambda_init=lambda_init),
        grid=(batch, DIFF_HEADS, nq),
        in_specs=[
            lam_spec, lam_spec, lam_spec, lam_spec,
            pl.BlockSpec((None, 1, DIFF_V_DIM), lambda b, h, i: (layer, 0, 0)),
            pl.BlockSpec((2, tq, HEAD_DIM), lambda b, h, i: (h, b * nq + i, 0)),
            pl.BlockSpec((2, nq, tq, HEAD_DIM), lambda b, h, i: (h, b, 0, 0)),
            pl.BlockSpec((None, nq, DIFF_V_DIM, tq), lambda b, h, i: (h, b, 0, 0)),
        ],
        out_specs=pl.BlockSpec((tq, DIFF_V_DIM), lambda b, h, i: (b * nq + i, h)),
        out_shape=jax.ShapeDtypeStruct((t, DIFF_V_WIDTH), BF16),
        scratch_shapes=[pltpu.VMEM((2, tq, tq), F32), pltpu.VMEM((2, DIFF_V_DIM, tq), F32)],
        compiler_params=_params("parallel", "parallel", "arbitrary"),
        name="diff_attention",
    )(*lams, sub_gain, q, k, v_t)


def _merge_kernel(h_ref, g_ref, wga_ref, wgb_ref, ya_ref, yb_ref, wa_ref, wb_ref, o_ref, u_ref):
    @pl.when(pl.program_id(1) == 0)
    def _():
        u_ref[...] = _rms(h_ref[...], g_ref[...]).astype(BF16)

    u = u_ref[...]
    ya = ya_ref[...]
    yb = yb_ref[...]
    for c0 in range(0, o_ref.shape[1], MXU_DIM):
        cols = slice(c0, c0 + MXU_DIM)
        merged = (jax.nn.sigmoid(_dot(u, wga_ref[:, cols])) * _dot(ya, wa_ref[:, cols])
                  + jax.nn.sigmoid(_dot(u, wgb_ref[:, cols])) * _dot(yb, wb_ref[:, cols]))
        o_ref[:, cols] = merged.astype(BF16)


def _merge(h, gain, w_in, ya, yb, wa, wb, layer, *, tm=512, tn=512):
    t = h.shape[0]
    ga0 = OFF_G_A // tn
    gb0 = OFF_G_B // tn
    return pl.pallas_call(
        _merge_kernel,
        grid=(t // tm, D_MODEL // tn),
        in_specs=[
            pl.BlockSpec((tm, D_MODEL), lambda i, j: (i, 0)),
            pl.BlockSpec((None, 1, D_MODEL), lambda i, j: (layer, 0, 0)),
            pl.BlockSpec((None, D_MODEL, tn), lambda i, j: (layer, 0, ga0 + j)),
            pl.BlockSpec((None, D_MODEL, tn), lambda i, j: (layer, 0, gb0 + j)),
            pl.BlockSpec((tm, SB_WIDTH), lambda i, j: (i, 0)),
            pl.BlockSpec((tm, DIFF_V_WIDTH), lambda i, j: (i, 0)),
            pl.BlockSpec((None, SB_WIDTH, tn), lambda i, j: (layer, 0, j)),
            pl.BlockSpec((None, DIFF_V_WIDTH, tn), lambda i, j: (layer, 0, j)),
        ],
        out_specs=pl.BlockSpec((tm, tn), lambda i, j: (i, j)),
        out_shape=jax.ShapeDtypeStruct((t, D_MODEL), BF16),
        scratch_shapes=[pltpu.VMEM((tm, D_MODEL), BF16)],
        compiler_params=_params("parallel", "arbitrary"),
        name="merge",
    )(h, gain, w_in, w_in, ya, yb, wa, wb)


def _out_kernel(h_ref, m_ref, w_ref, o_ref):
    o_ref[...] = h_ref[...] + _dot(m_ref[...], w_ref[...])


def _out_proj(h, merged, w_out, layer, *, tm=1024, tn=1024):
    t = h.shape[0]
    return pl.pallas_call(
        _out_kernel,
        grid=(t // tm, D_MODEL // tn),
        in_specs=[
            pl.BlockSpec((tm, tn), lambda i, j: (i, j)),
            pl.BlockSpec((tm, D_MODEL), lambda i, j: (i, 0)),
            pl.BlockSpec((None, D_MODEL, tn), lambda i, j: (layer, 0, j)),
        ],
        out_specs=pl.BlockSpec((tm, tn), lambda i, j: (i, j)),
        out_shape=jax.ShapeDtypeStruct((t, D_MODEL), F32),
        compiler_params=_params("parallel", "arbitrary"),
        name="out_proj",
    )(h, merged, w_out)


def _ple_kernel(h_ref, g_ref, wg_ref, p_ref, wp_ref, og_ref, o_ref):
    h = h_ref[...]
    u = _rms(h, g_ref[...]).astype(BF16)
    gate = jax.nn.sigmoid(_dot(u, wg_ref[...]))
    emb = _rms(_dot(p_ref[...].astype(BF16), wp_ref[...]), og_ref[...])
    o_ref[...] = h + gate * emb


def _ple(h, gain, w_gate, p, w_proj, out_gain, layer, *, tm=512):
    t = h.shape[0]
    nt = t // tm
    return pl.pallas_call(
        _ple_kernel,
        grid=(nt,),
        in_specs=[
            pl.BlockSpec((tm, D_MODEL), lambda i: (i, 0)),
            pl.BlockSpec((None, 1, D_MODEL), lambda i: (layer, 0, 0)),
            pl.BlockSpec((None, D_MODEL, D_MODEL), lambda i: (layer, 0, 0)),
            pl.BlockSpec((None, tm, PLE_DIM), lambda i: (layer, i, 0)),
            pl.BlockSpec((None, PLE_DIM, D_MODEL), lambda i: (layer, 0, 0)),
            pl.BlockSpec((None, 1, D_MODEL), lambda i: (layer, 0, 0)),
        ],
        out_specs=pl.BlockSpec((tm, D_MODEL), lambda i: (i, 0)),
        out_shape=jax.ShapeDtypeStruct((t, D_MODEL), F32),
        compiler_params=_params("parallel"),
        name="ple",
    )(h, gain, w_gate, p, w_proj, out_gain)


def _rotary_layout():
    half = ROT_DIM // 2
    mid = HEAD_DIM // 2
    return (list(range(half)) + list(range(ROT_DIM, ROT_DIM + mid - half))
            + list(range(half, ROT_DIM)) + list(range(ROT_DIM + mid - half, HEAD_DIM)))


def _rope_tables(seq):
    pos = jnp.arange(seq, dtype=F32)
    inv_freq = ROPE_THETA ** (-jnp.arange(0, ROT_DIM, 2, dtype=F32) / ROT_DIM)
    ang = pos[:, None] * inv_freq[None, :]
    cos, sin = jnp.cos(ang), jnp.sin(ang)
    half = ROT_DIM // 2
    gap = HEAD_DIM // 2 - half
    cos_t = jnp.concatenate([cos, jnp.ones((seq, gap), F32), cos, jnp.ones((seq, gap), F32)], axis=1)
    sin_t = jnp.concatenate([-sin, jnp.zeros((seq, gap), F32), sin, jnp.zeros((seq, gap), F32)], axis=1)
    return cos_t, sin_t


def kernel(x, p, ffn1_norm, ffn1_w_gu, ffn1_w_down, mix_norm, w_in, diff_q_norm, diff_k_norm,
           diff_lambda_q1, diff_lambda_k1, diff_lambda_q2, diff_lambda_k2, diff_sub_norm,
           w_branch_a, w_branch_b, w_out, ffn2_norm, ffn2_w_gu, ffn2_w_down, ple_norm,
           ple_w_gate, ple_w_proj, ple_out_norm):
    batch, seq, _ = x.shape
    t = batch * seq
    row = lambda a: a.reshape(DEPTH, 1, a.shape[-1])
    bf = lambda a: a.astype(BF16)

    ffn1_w_gu, ffn1_w_down, ffn2_w_gu, ffn2_w_down = map(
        bf, (ffn1_w_gu, ffn1_w_down, ffn2_w_gu, ffn2_w_down))
    w_in, w_branch_a, w_branch_b, w_out = map(bf, (w_in, w_branch_a, w_branch_b, w_out))
    ple_w_gate, ple_w_proj = bf(ple_w_gate), bf(ple_w_proj)
    ffn1_norm, mix_norm, ffn2_norm, ple_norm, ple_out_norm, diff_sub_norm = map(
        row, (ffn1_norm, mix_norm, ffn2_norm, ple_norm, ple_out_norm, diff_sub_norm))
    lams = tuple(map(row, (diff_lambda_q1, diff_lambda_k1, diff_lambda_q2, diff_lambda_k2)))
    layout = jnp.asarray(_rotary_layout())
    qk_gain = jnp.stack([row(diff_q_norm), row(diff_k_norm)])[..., layout]
    w_dfqk = w_in[:, :, OFF_DF_Q:OFF_DF_V].reshape(DEPTH, D_MODEL, -1, HEAD_DIM)[..., layout]
    w_dfqk = w_dfqk.reshape(DEPTH, D_MODEL, 2 * DIFF_QK_WIDTH)
    cos_t, sin_t = _rope_tables(seq)
    p = p.reshape(DEPTH, t, PLE_DIM)

    h = x.reshape(t, D_MODEL)
    for layer in range(DEPTH):
        lambda_init = 0.8 - 0.6 * math.exp(-0.3 * layer)
        h = _ffn(h, ffn1_norm, ffn1_w_gu, ffn1_w_down, layer)

        sb_qkv = _proj_heads(h, mix_norm, w_in, layer, 0, 3 * SB_WIDTH, HEAD_DIM, -QK_SCALE)
        df_qk = _proj_qk(h, mix_norm, w_dfqk, qk_gain, cos_t, sin_t, layer, seq)
        df_v = _proj_heads(h, mix_norm, w_in, layer, OFF_DF_V, DIFF_V_WIDTH, DIFF_V_DIM, None)
        y_a = _sb_attention(sb_qkv, batch, seq)
        y_b = _diff_attention(lams, diff_sub_norm, df_qk, df_v, layer, batch, seq, lambda_init)
        merged = _merge(h, mix_norm, w_in, y_a, y_b, w_branch_a, w_branch_b, layer)
        h = _out_proj(h, merged, w_out, layer)

        h = _ffn(h, ffn2_norm, ffn2_w_gu, ffn2_w_down, layer)
        h = _ple(h, ple_norm, ple_w_gate, p, ple_w_proj, ple_out_norm, layer)
    return h.reshape(batch, seq, D_MODEL)
```

```python
import functools
import math

import jax
import jax.numpy as jnp
from jax import lax
from jax.experimental import pallas as pl
from jax.experimental.pallas import tpu as pltpu

D_MODEL = 2048
DEPTH = 2
HEAD_DIM = 128
SB_HEADS = 8
DIFF_HEADS = 4
DIFF_V_DIM = 2 * HEAD_DIM
SB_WIDTH = SB_HEADS * HEAD_DIM
DIFF_QK_WIDTH = DIFF_HEADS * 2 * HEAD_DIM
DIFF_V_WIDTH = DIFF_HEADS * DIFF_V_DIM
D_FF = 5632
PLE_DIM = 256
ROPE_THETA = 500000.0
ROT_DIM = HEAD_DIM // 4
EPS = 1e-6
FFN_RES_WEIGHT = 0.5
QK_SCALE = HEAD_DIM ** -0.5 * math.log2(math.e)

OFF_DF_Q = 3 * SB_WIDTH
OFF_DF_V = OFF_DF_Q + 2 * DIFF_QK_WIDTH
OFF_G_A = OFF_DF_V + DIFF_V_WIDTH
OFF_G_B = OFF_G_A + D_MODEL

F32 = jnp.float32
BF16 = jnp.bfloat16

VMEM_LIMIT_BYTES = 56 * 1024 * 1024
SUBLANES = 8
MXU_DIM = 256
ATTN_BLOCK = 512
MASK_VALUE = -1e30

MIX_BLOCK = 1024
SB_Q_BLOCK, SB_K_BLOCK, SB_V_BLOCK, DF_Q_BLOCK, DF_K_BLOCK, DF_V_BLOCK = range(6)
HEADS_PER_MIX_BLOCK = MIX_BLOCK // HEAD_DIM


def _params(*sem):
    return pltpu.CompilerParams(dimension_semantics=sem, vmem_limit_bytes=VMEM_LIMIT_BYTES)


def _rms(x, g):
    ms = jnp.mean(x * x, axis=-1, keepdims=True)
    return x * lax.rsqrt(ms + EPS) * g


def _dot(a, b):
    return jnp.dot(a, b, preferred_element_type=F32)


def _dot_nt(a, b):
    return lax.dot_general(a, b, (((1,), (1,)), ((), ())), preferred_element_type=F32)


def _dot_tn(a, b):
    return lax.dot_general(a, b, (((0,), (0,)), ((), ())), preferred_element_type=F32)


def _block_position(idx):
    seg = ATTN_BLOCK // SUBLANES
    return (idx % SUBLANES) * seg + (seg - 1 - idx // SUBLANES)


def _ffn_kernel(h_ref, g_ref, wg_ref, wu_ref, wd_ref, o_ref, u_ref, a_ref, *, tn, nf):
    k = pl.program_id(1)

    def hidden():
        u = u_ref[...]
        gate = _dot(u, wg_ref[...])
        up = _dot(u, wu_ref[...])
        a_ref[...] = (gate * jax.nn.sigmoid(gate) * up).astype(BF16)

    def down():
        a = a_ref[...]
        for n in range(D_MODEL // tn):
            cols = slice(n * tn, (n + 1) * tn)
            o_ref[:, cols] += FFN_RES_WEIGHT * _dot(a, wd_ref[:, cols])

    @pl.when(k == 0)
    def _():
        h = h_ref[...]
        u_ref[...] = _rms(h, g_ref[...]).astype(BF16)
        o_ref[...] = h
        hidden()

    @pl.when(jnp.logical_and(k > 0, k < nf))
    def _():
        down()
        hidden()

    @pl.when(k == nf)
    def _():
        down()


def _ffn(h, gain, w_gu, w_down, layer, *, tm=512, tf=512, tn=512):
    t = h.shape[0]
    nf = D_FF // tf
    hidden_chunk = lambda k: jnp.minimum(k, nf - 1)
    down_chunk = lambda k: jnp.maximum(k - 1, 0)
    return pl.pallas_call(
        functools.partial(_ffn_kernel, tn=tn, nf=nf),
        grid=(t // tm, nf + 1),
        in_specs=[
            pl.BlockSpec((tm, D_MODEL), lambda i, k: (i, 0)),
            pl.BlockSpec((None, 1, D_MODEL), lambda i, k: (layer, 0, 0)),
            pl.BlockSpec((None, D_MODEL, tf), lambda i, k: (layer, 0, hidden_chunk(k))),
            pl.BlockSpec((None, D_MODEL, tf), lambda i, k: (layer, 0, hidden_chunk(k) + nf)),
            pl.BlockSpec((None, tf, D_MODEL), lambda i, k: (layer, down_chunk(k), 0)),
        ],
        out_specs=pl.BlockSpec((tm, D_MODEL), lambda i, k: (i, 0)),
        out_shape=jax.ShapeDtypeStruct((t, D_MODEL), F32),
        scratch_shapes=[pltpu.VMEM((tm, D_MODEL), BF16), pltpu.VMEM((tm, tf), BF16)],
        compiler_params=_params("parallel", "arbitrary"),
        name="ffn",
    )(h, gain, w_gu, w_gu, w_down)


def _mixer_proj_kernel(h_ref, g_ref, w_ref, qkg_ref, cos_ref, sin_ref, o_ref, u_ref, *, tm):
    j = pl.program_id(1)

    @pl.when(j == 0)
    def _():
        u_ref[...] = _rms(h_ref[...], g_ref[...]).astype(BF16)

    u = u_ref[...]

    def store(head, y):
        for b in range(tm // ATTN_BLOCK):
            o_ref[head, b] = y[:, b * ATTN_BLOCK:(b + 1) * ATTN_BLOCK].astype(BF16)

    def run(finish):
        starts = list(range(0, MIX_BLOCK, MXU_DIM))
        r = _dot_nt(w_ref[starts[0]:starts[0] + MXU_DIM, :], u)
        for prev, r0 in zip(starts, starts[1:]):
            r_next = _dot_nt(w_ref[r0:r0 + MXU_DIM, :], u)
            finish(r, prev)
            r = r_next
        finish(r, starts[-1])

    def heads_of(r, r0):
        for c in range(MXU_DIM // HEAD_DIM):
            yield r0 // HEAD_DIM + c, r[c * HEAD_DIM:(c + 1) * HEAD_DIM, :]

    is_rotary = jnp.logical_or(j == DF_Q_BLOCK, j == DF_K_BLOCK)

    @pl.when(jnp.logical_not(is_rotary))
    def _():
        scale = jnp.where(j == SB_Q_BLOCK, -QK_SCALE, 1.0).astype(F32)

        def finish(r, r0):
            for head, x in heads_of(r, r0):
                store(head, x * scale)

        run(finish)

    @pl.when(is_rotary)
    def _():
        gain = jnp.concatenate([qkg_ref[...]] * (tm // HEAD_DIM), axis=1)
        gain = gain * jnp.where(j == DF_Q_BLOCK, QK_SCALE, 1.0).astype(F32)
        cos = cos_ref[...]
        sin = sin_ref[...]
        half = ROT_DIM // 2

        def finish(r, r0):
            for head, x in heads_of(r, r0):
                ms = jnp.mean(x * x, axis=0, keepdims=True)
                y = x * lax.rsqrt(ms + EPS) * gain
                y1, y2 = y[:half], y[half:ROT_DIM]
                store(head, jnp.concatenate(
                    [y1 * cos - y2 * sin, y2 * cos + y1 * sin, y[ROT_DIM:]], axis=0))

        run(finish)


def _mixer_proj(h, gain, w_t, qk_gain, cos_t, sin_t, layer, seq, *, tm=1024):
    t = h.shape[0]
    nblocks = w_t.shape[1] // MIX_BLOCK
    nseq = seq // tm
    return pl.pallas_call(
        functools.partial(_mixer_proj_kernel, tm=tm),
        grid=(t // tm, nblocks),
        in_specs=[
            pl.BlockSpec((tm, D_MODEL), lambda i, j: (i, 0)),
            pl.BlockSpec((None, 1, D_MODEL), lambda i, j: (layer, 0, 0)),
            pl.BlockSpec((None, MIX_BLOCK, D_MODEL), lambda i, j: (layer, j, 0)),
            pl.BlockSpec((None, None, HEAD_DIM, HEAD_DIM),
                         lambda i, j: (jnp.clip(j - DF_Q_BLOCK, 0, 1), layer, 0, 0)),
            pl.BlockSpec((ROT_DIM // 2, tm), lambda i, j: (0, i % nseq)),
            pl.BlockSpec((ROT_DIM // 2, tm), lambda i, j: (0, i % nseq)),
        ],
        out_specs=pl.BlockSpec((HEADS_PER_MIX_BLOCK, tm // ATTN_BLOCK, HEAD_DIM, ATTN_BLOCK),
                               lambda i, j: (j, i, 0, 0)),
        out_shape=jax.ShapeDtypeStruct(
            (nblocks * HEADS_PER_MIX_BLOCK, t // ATTN_BLOCK, HEAD_DIM, ATTN_BLOCK), BF16),
        scratch_shapes=[pltpu.VMEM((tm, D_MODEL), BF16)],
        compiler_params=_params("parallel", "arbitrary"),
        name="mixer_proj",
    )(h, gain, w_t, qk_gain, cos_t, sin_t)


def _sb_kernel(q_ref, k_ref, v_ref, o_ref, z_scr, w_scr, acc_scr):
    qi = pl.program_id(2)
    tq = tk = ATTN_BLOCK
    q_t = q_ref[...]
    ngroups = tk // SUBLANES
    seg = tk // SUBLANES
    group = lambda g: slice(g * SUBLANES, (g + 1) * SUBLANES)

    def logits(kb):
        z_scr[...] = _dot_tn(k_ref[kb], q_t)

    def weights(masked):
        if masked:
            sub = lax.broadcasted_iota(jnp.int32, (SUBLANES, tq), 0)
            t_pos = _block_position(lax.broadcasted_iota(jnp.int32, (SUBLANES, tq), 1))
        run = jnp.ones((SUBLANES, tq), F32)
        for g in range(ngroups):
            beta = 1.0 / (1.0 + jnp.exp2(z_scr[group(g), :]))
            keep = 1.0 - beta
            if masked:
                mask = sub * seg + (seg - 1 - g) < t_pos
                beta = jnp.where(mask, beta, 0.0)
                keep = jnp.where(mask, keep, 1.0)
            w_scr[group(g), :] = beta * run
            run = run * keep
        return run

    def accumulate(kb, totals, carry):
        offs = [None] * SUBLANES
        for i in reversed(range(SUBLANES)):
            offs[i] = carry
            carry = carry * totals[i:i + 1, :]
        off = jnp.concatenate(offs, axis=0)
        attn = jnp.concatenate([w_scr[group(g), :] * off for g in range(ngroups)],
                               axis=0).astype(BF16)
        acc_scr[...] += _dot(v_ref[kb], attn)
        return carry

    def body(it, state):
        totals, carry = state
        kb = qi - it
        carry = accumulate(kb, totals, carry)
        totals = weights(False)
        logits(jnp.maximum(kb - 2, 0))
        return totals, carry

    acc_scr[...] = jnp.zeros_like(acc_scr)
    logits(qi)
    totals = weights(True)
    logits(jnp.maximum(qi - 1, 0))
    totals, carry = lax.fori_loop(0, qi, body, (totals, jnp.ones((1, tq), F32)))
    accumulate(0, totals, carry)
    o_ref[...] = acc_scr[...].T.astype(BF16)


def _sb_attention(qkv, batch, seq):
    nq = seq // ATTN_BLOCK
    t = batch * seq
    head_keys = (None, nq, HEAD_DIM, ATTN_BLOCK)
    first = lambda block: block * HEADS_PER_MIX_BLOCK
    return pl.pallas_call(
        _sb_kernel,
        grid=(batch, SB_HEADS, nq),
        in_specs=[
            pl.BlockSpec((None, None, HEAD_DIM, ATTN_BLOCK),
                         lambda b, h, i: (first(SB_Q_BLOCK) + h, b * nq + i, 0, 0)),
            pl.BlockSpec(head_keys, lambda b, h, i: (first(SB_K_BLOCK) + h, b, 0, 0)),
            pl.BlockSpec(head_keys, lambda b, h, i: (first(SB_V_BLOCK) + h, b, 0, 0)),
        ],
        out_specs=pl.BlockSpec((ATTN_BLOCK, HEAD_DIM), lambda b, h, i: (b * nq + i, h)),
        out_shape=jax.ShapeDtypeStruct((t, SB_WIDTH), BF16),
        scratch_shapes=[pltpu.VMEM((ATTN_BLOCK, ATTN_BLOCK), F32),
                        pltpu.VMEM((ATTN_BLOCK, ATTN_BLOCK), F32),
                        pltpu.VMEM((HEAD_DIM, ATTN_BLOCK), F32)],
        compiler_params=_params("parallel", "parallel", "arbitrary"),
        name="sb_attention",
    )(qkv, qkv, qkv)


def _diff_kernel(lq1_ref, lk1_ref, lq2_ref, lk2_ref, sub_ref, q_ref, k_ref, v_ref, o_ref,
                 s_scr, acc_scr, *, lambda_init):
    qi = pl.program_id(2)
    tq = tk = ATTN_BLOCK

    def logits(kb, masked):
        maxima = []
        for half in range(2):
            s = _dot_tn(k_ref[half, kb], q_ref[half])
            if masked:
                s_pos = _block_position(lax.broadcasted_iota(jnp.int32, (tk, tq), 0))
                t_pos = _block_position(lax.broadcasted_iota(jnp.int32, (tk, tq), 1))
                s = jnp.where(s_pos <= t_pos, s, MASK_VALUE)
            s_scr[half] = s
            maxima.append(jnp.max(s, axis=0, keepdims=True))
        return tuple(maxima)

    def accumulate(kb, maxima, stats):
        v_t = jnp.concatenate([v_ref[0, kb], v_ref[1, kb]], axis=0)
        new_stats = []
        for half in range(2):
            m, l = stats[half]
            m_new = jnp.maximum(m, maxima[half])
            alpha = jnp.exp2(m - m_new)
            p = jnp.exp2(s_scr[half] - m_new)
            l = alpha * l + jnp.sum(p, axis=0, keepdims=True)
            acc_scr[half] = alpha * acc_scr[half] + _dot(v_t, p.astype(BF16))
            new_stats.append((m_new, l))
        return tuple(new_stats)

    def body(it, state):
        maxima, stats = state
        kb = qi - it
        stats = accumulate(kb, maxima, stats)
        return logits(kb - 1, False), stats

    init = (jnp.full((1, tq), MASK_VALUE, F32), jnp.zeros((1, tq), F32))
    acc_scr[...] = jnp.zeros_like(acc_scr)
    maxima, stats = lax.fori_loop(0, qi, body, (logits(qi, True), (init, init)))
    (_, l1), (_, l2) = accumulate(0, maxima, stats)

    lam = (jnp.exp(jnp.sum(lq1_ref[...] * lk1_ref[...], axis=1, keepdims=True))
           - jnp.exp(jnp.sum(lq2_ref[...] * lk2_ref[...], axis=1, keepdims=True))
           + lambda_init)
    y = (acc_scr[0] / l1 - lam * (acc_scr[1] / l2)).T
    o_ref[...] = (_rms(y, sub_ref[...]) * (1.0 - lambda_init)).astype(BF16)


def _diff_attention(lams, sub_gain, qkv, layer, batch, seq, lambda_init):
    nq = seq // ATTN_BLOCK
    t = batch * seq
    pair_keys = (2, nq, HEAD_DIM, ATTN_BLOCK)
    first_pair = lambda block: block * HEADS_PER_MIX_BLOCK // 2
    lam_spec = pl.BlockSpec((None, 1, HEAD_DIM), lambda b, h, i: (layer, 0, 0))
    return pl.pallas_call(
        functools.partial(_diff_kernel, lambda_init=lambda_init),
        grid=(batch, DIFF_HEADS, nq),
        in_specs=[
            lam_spec, lam_spec, lam_spec, lam_spec,
            pl.BlockSpec((None, 1, DIFF_V_DIM), lambda b, h, i: (layer, 0, 0)),
            pl.BlockSpec((2, None, HEAD_DIM, ATTN_BLOCK),
                         lambda b, h, i: (first_pair(DF_Q_BLOCK) + h, b * nq + i, 0, 0)),
            pl.BlockSpec(pair_keys, lambda b, h, i: (first_pair(DF_K_BLOCK) + h, b, 0, 0)),
            pl.BlockSpec(pair_keys, lambda b, h, i: (first_pair(DF_V_BLOCK) + h, b, 0, 0)),
        ],
        out_specs=pl.BlockSpec((ATTN_BLOCK, DIFF_V_DIM), lambda b, h, i: (b * nq + i, h)),
        out_shape=jax.ShapeDtypeStruct((t, DIFF_V_WIDTH), BF16),
        scratch_shapes=[pltpu.VMEM((2, ATTN_BLOCK, ATTN_BLOCK), F32),
                        pltpu.VMEM((2, DIFF_V_DIM, ATTN_BLOCK), F32)],
        compiler_params=_params("parallel", "parallel", "arbitrary"),
        name="diff_attention",
    )(*lams, sub_gain, qkv, qkv, qkv)


def _merge_kernel(h_ref, g_ref, wga_ref, wgb_ref, ya_ref, yb_ref, wa_ref, wb_ref, o_ref, u_ref):
    @pl.when(pl.program_id(1) == 0)
    def _():
        u_ref[...] = _rms(h_ref[...], g_ref[...]).astype(BF16)

    u = u_ref[...]
    merged = (jax.nn.sigmoid(_dot(u, wga_ref[...])) * _dot(ya_ref[...], wa_ref[...])
              + jax.nn.sigmoid(_dot(u, wgb_ref[...])) * _dot(yb_ref[...], wb_ref[...]))
    o_ref[...] = merged.astype(BF16)


def _merge(h, gain, w_in, ya, yb, wa, wb, layer, *, tm=512, tn=512):
    t = h.shape[0]
    ga0 = OFF_G_A // tn
    gb0 = OFF_G_B // tn
    return pl.pallas_call(
        _merge_kernel,
        grid=(t // tm, D_MODEL // tn),
        in_specs=[
            pl.BlockSpec((tm, D_MODEL), lambda i, j: (i, 0)),
            pl.BlockSpec((None, 1, D_MODEL), lambda i, j: (layer, 0, 0)),
            pl.BlockSpec((None, D_MODEL, tn), lambda i, j: (layer, 0, ga0 + j)),
            pl.BlockSpec((None, D_MODEL, tn), lambda i, j: (layer, 0, gb0 + j)),
            pl.BlockSpec((tm, SB_WIDTH), lambda i, j: (i, 0)),
            pl.BlockSpec((tm, DIFF_V_WIDTH), lambda i, j: (i, 0)),
            pl.BlockSpec((None, SB_WIDTH, tn), lambda i, j: (layer, 0, j)),
            pl.BlockSpec((None, DIFF_V_WIDTH, tn), lambda i, j: (layer, 0, j)),
        ],
        out_specs=pl.BlockSpec((tm, tn), lambda i, j: (i, j)),
        out_shape=jax.ShapeDtypeStruct((t, D_MODEL), BF16),
        scratch_shapes=[pltpu.VMEM((tm, D_MODEL), BF16)],
        compiler_params=_params("parallel", "arbitrary"),
        name="merge",
    )(h, gain, w_in, w_in, ya, yb, wa, wb)


def _out_kernel(h_ref, m_ref, w_ref, o_ref):
    o_ref[...] = h_ref[...] + _dot(m_ref[...], w_ref[...])


def _out_proj(h, merged, w_out, layer, *, tm=1024, tn=1024):
    t = h.shape[0]
    return pl.pallas_call(
        _out_kernel,
        grid=(t // tm, D_MODEL // tn),
        in_specs=[
            pl.BlockSpec((tm, tn), lambda i, j: (i, j)),
            pl.BlockSpec((tm, D_MODEL), lambda i, j: (i, 0)),
            pl.BlockSpec((None, D_MODEL, tn), lambda i, j: (layer, 0, j)),
        ],
        out_specs=pl.BlockSpec((tm, tn), lambda i, j: (i, j)),
        out_shape=jax.ShapeDtypeStruct((t, D_MODEL), F32),
        compiler_params=_params("parallel", "arbitrary"),
        name="out_proj",
    )(h, merged, w_out)


def _ple_kernel(h_ref, g_ref, wg_ref, p_ref, wp_ref, og_ref, o_ref):
    h = h_ref[...]
    u = _rms(h, g_ref[...]).astype(BF16)
    gate = jax.nn.sigmoid(_dot(u, wg_ref[...]))
    emb = _rms(_dot(p_ref[...].astype(BF16), wp_ref[...]), og_ref[...])
    o_ref[...] = h + gate * emb


def _ple(h, gain, w_gate, p, w_proj, out_gain, layer, *, tm=512):
    t = h.shape[0]
    nt = t // tm
    return pl.pallas_call(
        _ple_kernel,
        grid=(nt,),
        in_specs=[
            pl.BlockSpec((tm, D_MODEL), lambda i: (i, 0)),
            pl.BlockSpec((None, 1, D_MODEL), lambda i: (layer, 0, 0)),
            pl.BlockSpec((None, D_MODEL, D_MODEL), lambda i: (layer, 0, 0)),
            pl.BlockSpec((None, tm, PLE_DIM), lambda i: (layer, i, 0)),
            pl.BlockSpec((None, PLE_DIM, D_MODEL), lambda i: (layer, 0, 0)),
            pl.BlockSpec((None, 1, D_MODEL), lambda i: (layer, 0, 0)),
        ],
        out_specs=pl.BlockSpec((tm, D_MODEL), lambda i: (i, 0)),
        out_shape=jax.ShapeDtypeStruct((t, D_MODEL), F32),
        compiler_params=_params("parallel"),
        name="ple",
    )(h, gain, w_gate, p, w_proj, out_gain)


def _to_kernel_order(a, axis):
    seg = ATTN_BLOCK // SUBLANES
    shape = a.shape
    a = a.reshape(shape[:axis] + (shape[axis] // ATTN_BLOCK, SUBLANES, seg) + shape[axis + 1:])
    a = jnp.swapaxes(jnp.flip(a, axis + 2), axis + 1, axis + 2)
    return a.reshape(shape)


def _from_kernel_order(a, axis):
    seg = ATTN_BLOCK // SUBLANES
    shape = a.shape
    a = a.reshape(shape[:axis] + (shape[axis] // ATTN_BLOCK, seg, SUBLANES) + shape[axis + 1:])
    a = jnp.flip(jnp.swapaxes(a, axis + 1, axis + 2), axis + 2)
    return a.reshape(shape)


def _rope_tables(seq):
    pos = jnp.arange(seq, dtype=F32)
    inv_freq = ROPE_THETA ** (-jnp.arange(0, ROT_DIM, 2, dtype=F32) / ROT_DIM)
    ang = _to_kernel_order(pos[:, None] * inv_freq[None, :], 0)
    return jnp.cos(ang).T, jnp.sin(ang).T


def kernel(x, p, ffn1_norm, ffn1_w_gu, ffn1_w_down, mix_norm, w_in, diff_q_norm, diff_k_norm,
           diff_lambda_q1, diff_lambda_k1, diff_lambda_q2, diff_lambda_k2, diff_sub_norm,
           w_branch_a, w_branch_b, w_out, ffn2_norm, ffn2_w_gu, ffn2_w_down, ple_norm,
           ple_w_gate, ple_w_proj, ple_out_norm):
    batch, seq, _ = x.shape
    t = batch * seq
    row = lambda a: a.reshape(DEPTH, 1, a.shape[-1])
    bf = lambda a: a.astype(BF16)

    ffn1_w_gu, ffn1_w_down, ffn2_w_gu, ffn2_w_down = map(
        bf, (ffn1_w_gu, ffn1_w_down, ffn2_w_gu, ffn2_w_down))
    w_in, w_branch_a, w_branch_b, w_out = map(bf, (w_in, w_branch_a, w_branch_b, w_out))
    ple_w_gate, ple_w_proj = bf(ple_w_gate), bf(ple_w_proj)
    w_qkv_t = jnp.swapaxes(w_in[:, :, :OFF_G_A], 1, 2)
    ffn1_norm, mix_norm, ffn2_norm, ple_norm, ple_out_norm, diff_sub_norm = map(
        row, (ffn1_norm, mix_norm, ffn2_norm, ple_norm, ple_out_norm, diff_sub_norm))
    lams = tuple(map(row, (diff_lambda_q1, diff_lambda_k1, diff_lambda_q2, diff_lambda_k2)))
    qk_gain = jnp.broadcast_to(jnp.stack([diff_q_norm, diff_k_norm])[..., None],
                               (2, DEPTH, HEAD_DIM, HEAD_DIM))
    cos_t, sin_t = _rope_tables(seq)
    p = _to_kernel_order(p.reshape(DEPTH, t, PLE_DIM), 1)

    h = _to_kernel_order(x.reshape(t, D_MODEL), 0)
    for layer in range(DEPTH):
        lambda_init = 0.8 - 0.6 * math.exp(-0.3 * layer)
        h = _ffn(h, ffn1_norm, ffn1_w_gu, ffn1_w_down, layer)

        qkv = _mixer_proj(h, mix_norm, w_qkv_t, qk_gain, cos_t, sin_t, layer, seq)
        y_a = _sb_attention(qkv, batch, seq)
        y_b = _diff_attention(lams, diff_sub_norm, qkv, layer, batch, seq, lambda_init)
        merged = _merge(h, mix_norm, w_in, y_a, y_b, w_branch_a, w_branch_b, layer)
        h = _out_proj(h, merged, w_out, layer)

        h = _ffn(h, ffn2_norm, ffn2_w_gu, ffn2_w_down, layer)
        h = _ple(h, ple_norm, ple_w_gate, p, ple_w_proj, ple_out_norm, layer)
    return _from_kernel_order(h, 0).reshape(batch, seq, D_MODEL)
```

```python
import functools
import math

import jax
import jax.numpy as jnp
from jax import lax
from jax.experimental import pallas as pl
from jax.experimental.pallas import tpu as pltpu

D_MODEL = 2048
DEPTH = 2
HEAD_DIM = 128
SB_HEADS = 8
DIFF_HEADS = 4
DIFF_V_DIM = 2 * HEAD_DIM
SB_WIDTH = SB_HEADS * HEAD_DIM
DIFF_QK_WIDTH = DIFF_HEADS * 2 * HEAD_DIM
DIFF_V_WIDTH = DIFF_HEADS * DIFF_V_DIM
D_FF = 5632
PLE_DIM = 256
ROPE_THETA = 500000.0
ROT_DIM = HEAD_DIM // 4
EPS = 1e-6
FFN_RES_WEIGHT = 0.5
QK_SCALE = HEAD_DIM ** -0.5 * math.log2(math.e)

OFF_DF_Q = 3 * SB_WIDTH
OFF_DF_V = OFF_DF_Q + 2 * DIFF_QK_WIDTH
OFF_G_A = OFF_DF_V + DIFF_V_WIDTH
OFF_G_B = OFF_G_A + D_MODEL

F32 = jnp.float32
BF16 = jnp.bfloat16

VMEM_LIMIT_BYTES = 56 * 1024 * 1024
SUBLANES = 8
MXU_DIM = 256
ATTN_BLOCK = 512
MASK_VALUE = -1e30

MIX_BLOCK = 1024
SB_Q_BLOCK, SB_K_BLOCK, SB_V_BLOCK, DF_Q_BLOCK, DF_K_BLOCK, DF_V_BLOCK = range(6)
HEADS_PER_MIX_BLOCK = MIX_BLOCK // HEAD_DIM


def _params(*sem):
    return pltpu.CompilerParams(dimension_semantics=sem, vmem_limit_bytes=VMEM_LIMIT_BYTES)


def _rms(x, g):
    ms = jnp.mean(x * x, axis=-1, keepdims=True)
    return x * lax.rsqrt(ms + EPS) * g


def _dot(a, b):
    return jnp.dot(a, b, preferred_element_type=F32)


def _dot_nt(a, b):
    return lax.dot_general(a, b, (((1,), (1,)), ((), ())), preferred_element_type=F32)


def _dot_tn(a, b):
    return lax.dot_general(a, b, (((0,), (0,)), ((), ())), preferred_element_type=F32)


def _flip_row_groups(x):
    n = x.shape[0] // SUBLANES
    return jnp.concatenate(
        [x[(n - 1 - g) * SUBLANES:(n - g) * SUBLANES] for g in range(n)], axis=0)


def _block_position(idx):
    seg = ATTN_BLOCK // SUBLANES
    return (idx % SUBLANES) * seg + (seg - 1 - idx // SUBLANES)


def _ffn_kernel(h_ref, g_ref, wg_ref, wu_ref, wd_ref, o_ref, u_ref, *, tn, segment_order_in):
    @pl.when(pl.program_id(1) == 0)
    def _():
        h = h_ref[...]
        if segment_order_in:
            h = _flip_row_groups(h)
        u_ref[...] = _rms(h, g_ref[...]).astype(BF16)
        o_ref[...] = h

    u = u_ref[...]
    gate = _dot(u, wg_ref[...])
    up = _dot(u, wu_ref[...])
    a = (gate * jax.nn.sigmoid(gate) * up).astype(BF16)
    for n in range(D_MODEL // tn):
        cols = slice(n * tn, (n + 1) * tn)
        o_ref[:, cols] += FFN_RES_WEIGHT * _dot(a, wd_ref[:, cols])


def _ffn(h, gain, w_gu, w_down, layer, *, segment_order_in=False, tf=512, tn=512):
    t = h.shape[0]
    tm = ATTN_BLOCK
    nf = D_FF // tf
    return pl.pallas_call(
        functools.partial(_ffn_kernel, tn=tn, segment_order_in=segment_order_in),
        grid=(t // tm, nf),
        in_specs=[
            pl.BlockSpec((tm, D_MODEL), lambda i, k: (i, 0)),
            pl.BlockSpec((None, 1, D_MODEL), lambda i, k: (layer, 0, 0)),
            pl.BlockSpec((None, D_MODEL, tf), lambda i, k: (layer, 0, k)),
            pl.BlockSpec((None, D_MODEL, tf), lambda i, k: (layer, 0, k + nf)),
            pl.BlockSpec((None, tf, D_MODEL), lambda i, k: (layer, k, 0)),
        ],
        out_specs=pl.BlockSpec((tm, D_MODEL), lambda i, k: (i, 0)),
        out_shape=jax.ShapeDtypeStruct((t, D_MODEL), F32),
        scratch_shapes=[pltpu.VMEM((tm, D_MODEL), BF16)],
        compiler_params=_params("parallel", "arbitrary"),
        name="ffn",
    )(h, gain, w_gu, w_gu, w_down)


def _mixer_proj_kernel(h_ref, g_ref, w_ref, qkg_ref, cos_ref, sin_ref, o_ref, u_ref, *, tm):
    j = pl.program_id(1)

    @pl.when(j == 0)
    def _():
        u_ref[...] = _rms(h_ref[...], g_ref[...]).astype(BF16)

    u = u_ref[...]

    def store(head, y):
        for b in range(tm // ATTN_BLOCK):
            o_ref[head, b] = y[:, b * ATTN_BLOCK:(b + 1) * ATTN_BLOCK].astype(BF16)

    def run(finish):
        starts = list(range(0, MIX_BLOCK, MXU_DIM))
        r = _dot_nt(w_ref[starts[0]:starts[0] + MXU_DIM, :], u)
        for prev, r0 in zip(starts, starts[1:]):
            r_next = _dot_nt(w_ref[r0:r0 + MXU_DIM, :], u)
            finish(r, prev)
            r = r_next
        finish(r, starts[-1])

    def heads_of(r, r0):
        for c in range(MXU_DIM // HEAD_DIM):
            yield r0 // HEAD_DIM + c, r[c * HEAD_DIM:(c + 1) * HEAD_DIM, :]

    is_rotary = jnp.logical_or(j == DF_Q_BLOCK, j == DF_K_BLOCK)

    @pl.when(jnp.logical_not(is_rotary))
    def _():
        scale = jnp.where(j == SB_Q_BLOCK, -QK_SCALE, 1.0).astype(F32)

        def finish(r, r0):
            for head, x in heads_of(r, r0):
                store(head, x * scale)

        run(finish)

    @pl.when(is_rotary)
    def _():
        gain = jnp.concatenate([qkg_ref[...]] * (tm // HEAD_DIM), axis=1)
        gain = gain * jnp.where(j == DF_Q_BLOCK, QK_SCALE, 1.0).astype(F32)
        cos = cos_ref[...]
        sin = sin_ref[...]
        half = ROT_DIM // 2

        def finish(r, r0):
            for head, x in heads_of(r, r0):
                ms = jnp.mean(x * x, axis=0, keepdims=True)
                y = x * lax.rsqrt(ms + EPS) * gain
                y1, y2 = y[:half], y[half:ROT_DIM]
                store(head, jnp.concatenate(
                    [y1 * cos - y2 * sin, y2 * cos + y1 * sin, y[ROT_DIM:]], axis=0))

        run(finish)


def _mixer_proj(h, gain, w_t, qk_gain, cos_t, sin_t, layer, seq, *, tm=1024):
    t = h.shape[0]
    nblocks = w_t.shape[1] // MIX_BLOCK
    nseq = seq // tm
    return pl.pallas_call(
        functools.partial(_mixer_proj_kernel, tm=tm),
        grid=(t // tm, nblocks),
        in_specs=[
            pl.BlockSpec((tm, D_MODEL), lambda i, j: (i, 0)),
            pl.BlockSpec((None, 1, D_MODEL), lambda i, j: (layer, 0, 0)),
            pl.BlockSpec((None, MIX_BLOCK, D_MODEL), lambda i, j: (layer, j, 0)),
            pl.BlockSpec((None, None, HEAD_DIM, HEAD_DIM),
                         lambda i, j: (jnp.clip(j - DF_Q_BLOCK, 0, 1), layer, 0, 0)),
            pl.BlockSpec((ROT_DIM // 2, tm), lambda i, j: (0, i % nseq)),
            pl.BlockSpec((ROT_DIM // 2, tm), lambda i, j: (0, i % nseq)),
        ],
        out_specs=pl.BlockSpec((HEADS_PER_MIX_BLOCK, tm // ATTN_BLOCK, HEAD_DIM, ATTN_BLOCK),
                               lambda i, j: (j, i, 0, 0)),
        out_shape=jax.ShapeDtypeStruct(
            (nblocks * HEADS_PER_MIX_BLOCK, t // ATTN_BLOCK, HEAD_DIM, ATTN_BLOCK), BF16),
        scratch_shapes=[pltpu.VMEM((tm, D_MODEL), BF16)],
        compiler_params=_params("parallel", "arbitrary"),
        name="mixer_proj",
    )(h, gain, w_t, qk_gain, cos_t, sin_t)


def _sb_kernel(q_ref, k_ref, v_ref, o_ref, z_scr, w_scr, acc_scr):
    tq = tk = ATTN_BLOCK
    ngroups = tk // SUBLANES
    seg = tk // SUBLANES
    group = lambda g: slice(g * SUBLANES, (g + 1) * SUBLANES)

    def logits(qi, kb):
        z_scr[...] = _dot_tn(k_ref[kb], q_ref[qi])

    def weights(masked):
        if masked:
            sub = lax.broadcasted_iota(jnp.int32, (SUBLANES, tq), 0)
            t_pos = _block_position(lax.broadcasted_iota(jnp.int32, (SUBLANES, tq), 1))
        run = jnp.ones((SUBLANES, tq), F32)
        for g in range(ngroups):
            beta = 1.0 / (1.0 + jnp.exp2(z_scr[group(g), :]))
            keep = 1.0 - beta
            if masked:
                mask = sub * seg + (seg - 1 - g) < t_pos
                beta = jnp.where(mask, beta, 0.0)
                keep = jnp.where(mask, keep, 1.0)
            w_scr[group(g), :] = beta * run
            run = run * keep
        return run

    def accumulate(kb, totals, carry):
        offs = [None] * SUBLANES
        for i in reversed(range(SUBLANES)):
            offs[i] = carry
            carry = carry * totals[i:i + 1, :]
        off = jnp.concatenate(offs, axis=0)
        attn = jnp.concatenate([w_scr[group(g), :] * off for g in range(ngroups)],
                               axis=0).astype(BF16)
        acc_scr[...] += _dot(v_ref[kb], attn)
        return carry

    def query_block(qi, _):
        def body(it, state):
            totals, carry = state
            kb = qi - it
            carry = accumulate(kb, totals, carry)
            totals = weights(False)
            logits(qi, jnp.maximum(kb - 2, 0))
            return totals, carry

        acc_scr[...] = jnp.zeros_like(acc_scr)
        logits(qi, qi)
        totals = weights(True)
        logits(qi, jnp.maximum(qi - 1, 0))
        totals, carry = lax.fori_loop(0, qi, body, (totals, jnp.ones((1, tq), F32)))
        accumulate(0, totals, carry)
        o_ref[pl.ds(pl.multiple_of(qi * tq, tq), tq), :] = acc_scr[...].T.astype(BF16)
        return 0

    lax.fori_loop(0, q_ref.shape[0], query_block, 0)


def _sb_attention(qkv, batch, seq):
    nq = seq // ATTN_BLOCK
    t = batch * seq
    head_seq = (None, nq, HEAD_DIM, ATTN_BLOCK)
    first = lambda block: block * HEADS_PER_MIX_BLOCK
    return pl.pallas_call(
        _sb_kernel,
        grid=(batch, SB_HEADS),
        in_specs=[
            pl.BlockSpec(head_seq, lambda b, h: (first(SB_Q_BLOCK) + h, b, 0, 0)),
            pl.BlockSpec(head_seq, lambda b, h: (first(SB_K_BLOCK) + h, b, 0, 0)),
            pl.BlockSpec(head_seq, lambda b, h: (first(SB_V_BLOCK) + h, b, 0, 0)),
        ],
        out_specs=pl.BlockSpec((seq, HEAD_DIM), lambda b, h: (b, h)),
        out_shape=jax.ShapeDtypeStruct((t, SB_WIDTH), BF16),
        scratch_shapes=[pltpu.VMEM((ATTN_BLOCK, ATTN_BLOCK), F32),
                        pltpu.VMEM((ATTN_BLOCK, ATTN_BLOCK), F32),
                        pltpu.VMEM((HEAD_DIM, ATTN_BLOCK), F32)],
        compiler_params=_params("parallel", "parallel"),
        name="sb_attention",
    )(qkv, qkv, qkv)


def _diff_kernel(lq1_ref, lk1_ref, lq2_ref, lk2_ref, sub_ref, q_ref, k_ref, v_ref, o_ref,
                 s_scr, acc_scr, *, lambda_init):
    tq = tk = ATTN_BLOCK

    def logits(qi, kb, masked):
        maxima = []
        for half in range(2):
            s = _dot_tn(k_ref[half, kb], q_ref[half, qi])
            if masked:
                s_pos = _block_position(lax.broadcasted_iota(jnp.int32, (tk, tq), 0))
                t_pos = _block_position(lax.broadcasted_iota(jnp.int32, (tk, tq), 1))
                s = jnp.where(s_pos <= t_pos, s, MASK_VALUE)
            s_scr[half] = s
            maxima.append(jnp.max(s, axis=0, keepdims=True))
        return tuple(maxima)

    def accumulate(kb, maxima, stats):
        v_t = jnp.concatenate([v_ref[0, kb], v_ref[1, kb]], axis=0)
        new_stats = []
        for half in range(2):
            m, l = stats[half]
            m_new = jnp.maximum(m, maxima[half])
            alpha = jnp.exp2(m - m_new)
            p = jnp.exp2(s_scr[half] - m_new)
            l = alpha * l + jnp.sum(p, axis=0, keepdims=True)
            acc_scr[half] = alpha * acc_scr[half] + _dot(v_t, p.astype(BF16))
            new_stats.append((m_new, l))
        return tuple(new_stats)

    lam = (jnp.exp(jnp.sum(lq1_ref[...] * lk1_ref[...], axis=1, keepdims=True))
           - jnp.exp(jnp.sum(lq2_ref[...] * lk2_ref[...], axis=1, keepdims=True))
           + lambda_init)

    def query_block(qi, _):
        def body(it, state):
            maxima, stats = state
            kb = qi - it
            stats = accumulate(kb, maxima, stats)
            return logits(qi, kb - 1, False), stats

        init = (jnp.full((1, tq), MASK_VALUE, F32), jnp.zeros((1, tq), F32))
        acc_scr[...] = jnp.zeros_like(acc_scr)
        maxima, stats = lax.fori_loop(0, qi, body, (logits(qi, qi, True), (init, init)))
        (_, l1), (_, l2) = accumulate(0, maxima, stats)
        y = (acc_scr[0] / l1 - lam * (acc_scr[1] / l2)).T
        o_ref[pl.ds(pl.multiple_of(qi * tq, tq), tq), :] = (
            _rms(y, sub_ref[...]) * (1.0 - lambda_init)).astype(BF16)
        return 0

    lax.fori_loop(0, q_ref.shape[1], query_block, 0)


def _diff_attention(lams, sub_gain, qkv, layer, batch, seq, lambda_init):
    nq = seq // ATTN_BLOCK
    t = batch * seq
    pair_seq = (2, nq, HEAD_DIM, ATTN_BLOCK)
    first_pair = lambda block: block * HEADS_PER_MIX_BLOCK // 2
    lam_spec = pl.BlockSpec((None, 1, HEAD_DIM), lambda b, h: (layer, 0, 0))
    return pl.pallas_call(
        functools.partial(_diff_kernel, lambda_init=lambda_init),
        grid=(batch, DIFF_HEADS),
        in_specs=[
            lam_spec, lam_spec, lam_spec, lam_spec,
            pl.BlockSpec((None, 1, DIFF_V_DIM), lambda b, h: (layer, 0, 0)),
            pl.BlockSpec(pair_seq, lambda b, h: (first_pair(DF_Q_BLOCK) + h, b, 0, 0)),
            pl.BlockSpec(pair_seq, lambda b, h: (first_pair(DF_K_BLOCK) + h, b, 0, 0)),
            pl.BlockSpec(pair_seq, lambda b, h: (first_pair(DF_V_BLOCK) + h, b, 0, 0)),
        ],
        out_specs=pl.BlockSpec((seq, DIFF_V_DIM), lambda b, h: (b, h)),
        out_shape=jax.ShapeDtypeStruct((t, DIFF_V_WIDTH), BF16),
        scratch_shapes=[pltpu.VMEM((2, ATTN_BLOCK, ATTN_BLOCK), F32),
                        pltpu.VMEM((2, DIFF_V_DIM, ATTN_BLOCK), F32)],
        compiler_params=_params("parallel", "parallel"),
        name="diff_attention",
    )(*lams, sub_gain, qkv, qkv, qkv)


def _merge_kernel(h_ref, g_ref, wga_ref, wgb_ref, ya_ref, yb_ref, wa_ref, wb_ref, o_ref, u_ref):
    @pl.when(pl.program_id(1) == 0)
    def _():
        u_ref[...] = _rms(h_ref[...], g_ref[...]).astype(BF16)

    u = u_ref[...]
    merged = (jax.nn.sigmoid(_dot(u, wga_ref[...])) * _dot(ya_ref[...], wa_ref[...])
              + jax.nn.sigmoid(_dot(u, wgb_ref[...])) * _dot(yb_ref[...], wb_ref[...]))
    o_ref[...] = merged.astype(BF16)


def _merge(h, gain, w_in, ya, yb, wa, wb, layer, *, tm=1024, tn=512):
    t = h.shape[0]
    ga0 = OFF_G_A // tn
    gb0 = OFF_G_B // tn
    return pl.pallas_call(
        _merge_kernel,
        grid=(t // tm, D_MODEL // tn),
        in_specs=[
            pl.BlockSpec((tm, D_MODEL), lambda i, j: (i, 0)),
            pl.BlockSpec((None, 1, D_MODEL), lambda i, j: (layer, 0, 0)),
            pl.BlockSpec((None, D_MODEL, tn), lambda i, j: (layer, 0, ga0 + j)),
            pl.BlockSpec((None, D_MODEL, tn), lambda i, j: (layer, 0, gb0 + j)),
            pl.BlockSpec((tm, SB_WIDTH), lambda i, j: (i, 0)),
            pl.BlockSpec((tm, DIFF_V_WIDTH), lambda i, j: (i, 0)),
            pl.BlockSpec((None, SB_WIDTH, tn), lambda i, j: (layer, 0, j)),
            pl.BlockSpec((None, DIFF_V_WIDTH, tn), lambda i, j: (layer, 0, j)),
        ],
        out_specs=pl.BlockSpec((tm, tn), lambda i, j: (i, j)),
        out_shape=jax.ShapeDtypeStruct((t, D_MODEL), BF16),
        scratch_shapes=[pltpu.VMEM((tm, D_MODEL), BF16)],
        compiler_params=_params("parallel", "arbitrary"),
        name="merge",
    )(h, gain, w_in, w_in, ya, yb, wa, wb)


def _out_kernel(h_ref, m_ref, w_ref, o_ref):
    o_ref[...] = h_ref[...] + _dot(m_ref[...], w_ref[...])


def _out_proj(h, merged, w_out, layer, *, tm=1024, tn=1024):
    t = h.shape[0]
    return pl.pallas_call(
        _out_kernel,
        grid=(t // tm, D_MODEL // tn),
        in_specs=[
            pl.BlockSpec((tm, tn), lambda i, j: (i, j)),
            pl.BlockSpec((tm, D_MODEL), lambda i, j: (i, 0)),
            pl.BlockSpec((None, D_MODEL, tn), lambda i, j: (layer, 0, j)),
        ],
        out_specs=pl.BlockSpec((tm, tn), lambda i, j: (i, j)),
        out_shape=jax.ShapeDtypeStruct((t, D_MODEL), F32),
        compiler_params=_params("parallel", "arbitrary"),
        name="out_proj",
    )(h, merged, w_out)


def _ple_kernel(h_ref, g_ref, wg_ref, p_ref, wp_ref, og_ref, o_ref, *, segment_order_out):
    h = h_ref[...]
    u = _rms(h, g_ref[...]).astype(BF16)
    gate = jax.nn.sigmoid(_dot(u, wg_ref[...]))
    p = _flip_row_groups(p_ref[...]).astype(BF16)
    out = h + gate * _rms(_dot(p, wp_ref[...]), og_ref[...])
    o_ref[...] = _flip_row_groups(out) if segment_order_out else out


def _ple(h, gain, w_gate, p, w_proj, out_gain, layer, *, segment_order_out=False):
    t = h.shape[0]
    tm = ATTN_BLOCK
    nt = t // tm
    return pl.pallas_call(
        functools.partial(_ple_kernel, segment_order_out=segment_order_out),
        grid=(nt,),
        in_specs=[
            pl.BlockSpec((tm, D_MODEL), lambda i: (i, 0)),
            pl.BlockSpec((None, 1, D_MODEL), lambda i: (layer, 0, 0)),
            pl.BlockSpec((None, D_MODEL, D_MODEL), lambda i: (layer, 0, 0)),
            pl.BlockSpec((None, tm, PLE_DIM), lambda i: (layer, i, 0)),
            pl.BlockSpec((None, PLE_DIM, D_MODEL), lambda i: (layer, 0, 0)),
            pl.BlockSpec((None, 1, D_MODEL), lambda i: (layer, 0, 0)),
        ],
        out_specs=pl.BlockSpec((tm, D_MODEL), lambda i: (i, 0)),
        out_shape=jax.ShapeDtypeStruct((t, D_MODEL), F32),
        compiler_params=_params("parallel"),
        name="ple",
    )(h, gain, w_gate, p, w_proj, out_gain)


def _to_segment_order(a, axis, reverse_positions=False):
    seg = ATTN_BLOCK // SUBLANES
    shape = a.shape
    a = a.reshape(shape[:axis] + (shape[axis] // ATTN_BLOCK, SUBLANES, seg) + shape[axis + 1:])
    if reverse_positions:
        a = jnp.flip(a, axis + 2)
    return jnp.swapaxes(a, axis + 1, axis + 2).reshape(shape)


def _from_segment_order(a, axis):
    seg = ATTN_BLOCK // SUBLANES
    shape = a.shape
    a = a.reshape(shape[:axis] + (shape[axis] // ATTN_BLOCK, seg, SUBLANES) + shape[axis + 1:])
    return jnp.swapaxes(a, axis + 1, axis + 2).reshape(shape)


def _rope_tables(seq):
    pos = jnp.arange(seq, dtype=F32)
    inv_freq = ROPE_THETA ** (-jnp.arange(0, ROT_DIM, 2, dtype=F32) / ROT_DIM)
    ang = _to_segment_order(pos[:, None] * inv_freq[None, :], 0, reverse_positions=True)
    return jnp.cos(ang).T, jnp.sin(ang).T


def kernel(x, p, ffn1_norm, ffn1_w_gu, ffn1_w_down, mix_norm, w_in, diff_q_norm, diff_k_norm,
           diff_lambda_q1, diff_lambda_k1, diff_lambda_q2, diff_lambda_k2, diff_sub_norm,
           w_branch_a, w_branch_b, w_out, ffn2_norm, ffn2_w_gu, ffn2_w_down, ple_norm,
           ple_w_gate, ple_w_proj, ple_out_norm):
    batch, seq, _ = x.shape
    t = batch * seq
    row = lambda a: a.reshape(DEPTH, 1, a.shape[-1])
    bf = lambda a: a.astype(BF16)

    ffn1_w_gu, ffn1_w_down, ffn2_w_gu, ffn2_w_down = map(
        bf, (ffn1_w_gu, ffn1_w_down, ffn2_w_gu, ffn2_w_down))
    w_in, w_branch_a, w_branch_b, w_out = map(bf, (w_in, w_branch_a, w_branch_b, w_out))
    ple_w_gate, ple_w_proj = bf(ple_w_gate), bf(ple_w_proj)
    w_qkv_t = jnp.swapaxes(w_in[:, :, :OFF_G_A], 1, 2)
    ffn1_norm, mix_norm, ffn2_norm, ple_norm, ple_out_norm, diff_sub_norm = map(
        row, (ffn1_norm, mix_norm, ffn2_norm, ple_norm, ple_out_norm, diff_sub_norm))
    lams = tuple(map(row, (diff_lambda_q1, diff_lambda_k1, diff_lambda_q2, diff_lambda_k2)))
    qk_gain = jnp.broadcast_to(jnp.stack([diff_q_norm, diff_k_norm])[..., None],
                               (2, DEPTH, HEAD_DIM, HEAD_DIM))
    cos_t, sin_t = _rope_tables(seq)
    p = _to_segment_order(p.reshape(DEPTH, t, PLE_DIM), 1)

    h = _to_segment_order(x.reshape(t, D_MODEL), 0)
    for layer in range(DEPTH):
        lambda_init = 0.8 - 0.6 * math.exp(-0.3 * layer)
        h = _ffn(h, ffn1_norm, ffn1_w_gu, ffn1_w_down, layer, segment_order_in=(layer == 0))

        qkv = _mixer_proj(h, mix_norm, w_qkv_t, qk_gain, cos_t, sin_t, layer, seq)
        y_a = _sb_attention(qkv, batch, seq)
        y_b = _diff_attention(lams, diff_sub_norm, qkv, layer, batch, seq, lambda_init)
        merged = _merge(h, mix_norm, w_in, y_a, y_b, w_branch_a, w_branch_b, layer)
        h = _out_proj(h, merged, w_out, layer)

        h = _ffn(h, ffn2_norm, ffn2_w_gu, ffn2_w_down, layer)
        h = _ple(h, ple_norm, ple_w_gate, p, ple_w_proj, ple_out_norm, layer,
                 segment_order_out=(layer == DEPTH - 1))
    return _from_segment_order(h, 0).reshape(batch, seq, D_MODEL)
```

```python
import functools
import math

import jax
import jax.numpy as jnp
from jax import lax
from jax.experimental import pallas as pl
from jax.experimental.pallas import tpu as pltpu

D_MODEL = 2048
DEPTH = 2
HEAD_DIM = 128
SB_HEADS = 8
DIFF_HEADS = 4
DIFF_V_DIM = 2 * HEAD_DIM
SB_WIDTH = SB_HEADS * HEAD_DIM
DIFF_QK_WIDTH = DIFF_HEADS * 2 * HEAD_DIM
DIFF_V_WIDTH = DIFF_HEADS * DIFF_V_DIM
D_FF = 5632
PLE_DIM = 256
ROPE_THETA = 500000.0
ROT_DIM = HEAD_DIM // 4
EPS = 1e-6
FFN_RES_WEIGHT = 0.5
QK_SCALE = HEAD_DIM ** -0.5 * math.log2(math.e)

OFF_DF_Q = 3 * SB_WIDTH
OFF_DF_V = OFF_DF_Q + 2 * DIFF_QK_WIDTH
OFF_G_A = OFF_DF_V + DIFF_V_WIDTH
OFF_G_B = OFF_G_A + D_MODEL

F32 = jnp.float32
BF16 = jnp.bfloat16

VMEM_LIMIT_BYTES = 56 * 1024 * 1024
SUBLANES = 8
MXU_DIM = 256
ATTN_BLOCK = 512
MASK_VALUE = -1e30

MIX_BLOCK = 1024
SB_Q_BLOCK, SB_K_BLOCK, SB_V_BLOCK, DF_Q_BLOCK, DF_K_BLOCK, DF_V_BLOCK = range(6)
HEADS_PER_MIX_BLOCK = MIX_BLOCK // HEAD_DIM


def _params(*sem):
    return pltpu.CompilerParams(dimension_semantics=sem, vmem_limit_bytes=VMEM_LIMIT_BYTES)


def _rms(x, g):
    ms = jnp.mean(x * x, axis=-1, keepdims=True)
    return x * lax.rsqrt(ms + EPS) * g


def _dot(a, b):
    return jnp.dot(a, b, preferred_element_type=F32)


def _dot_nt(a, b):
    return lax.dot_general(a, b, (((1,), (1,)), ((), ())), preferred_element_type=F32)


def _dot_tn_nt(a, b):
    return lax.dot_general(a, b, (((0,), (1,)), ((), ())), preferred_element_type=F32)


def _dot_tn(a, b):
    return lax.dot_general(a, b, (((0,), (0,)), ((), ())), preferred_element_type=F32)


def _flip_row_groups(x):
    n = x.shape[0] // SUBLANES
    return jnp.concatenate(
        [x[(n - 1 - g) * SUBLANES:(n - g) * SUBLANES] for g in range(n)], axis=0)


def _block_position(idx):
    seg = ATTN_BLOCK // SUBLANES
    return (idx % SUBLANES) * seg + (seg - 1 - idx // SUBLANES)


def _ffn_kernel(h_ref, g_ref, wg_ref, wu_ref, wd_ref, o_ref, u_ref, *, tn, segment_order_in):
    @pl.when(pl.program_id(1) == 0)
    def _():
        h = h_ref[...]
        if segment_order_in:
            h = jnp.concatenate([_flip_row_groups(h[b:b + ATTN_BLOCK])
                                 for b in range(0, h.shape[0], ATTN_BLOCK)], axis=0)
        u_ref[...] = _rms(h, g_ref[...]).astype(BF16)
        o_ref[...] = h

    u = u_ref[...]
    acts = []
    for c in range(0, wg_ref.shape[1], MXU_DIM):
        sub = slice(c, c + MXU_DIM)
        gate = _dot(u, wg_ref[:, sub])
        up = _dot(u, wu_ref[:, sub])
        acts.append((sub, (gate * jax.nn.sigmoid(gate) * up).astype(BF16)))
    for sub, a in acts:
        for n in range(D_MODEL // tn):
            cols = slice(n * tn, (n + 1) * tn)
            o_ref[:, cols] += FFN_RES_WEIGHT * _dot(a, wd_ref[sub, cols])


def _ffn(h, gain, w_gu, w_down, layer, *, segment_order_in=False, tm=1024, tf=512, tn=512):
    t = h.shape[0]
    assert tm % ATTN_BLOCK == 0
    nf = D_FF // tf
    return pl.pallas_call(
        functools.partial(_ffn_kernel, tn=tn, segment_order_in=segment_order_in),
        grid=(t // tm, nf),
        in_specs=[
            pl.BlockSpec((tm, D_MODEL), lambda i, k: (i, 0)),
            pl.BlockSpec((None, 1, D_MODEL), lambda i, k: (layer, 0, 0)),
            pl.BlockSpec((None, D_MODEL, tf), lambda i, k: (layer, 0, k)),
            pl.BlockSpec((None, D_MODEL, tf), lambda i, k: (layer, 0, k + nf)),
            pl.BlockSpec((None, tf, D_MODEL), lambda i, k: (layer, k, 0)),
        ],
        out_specs=pl.BlockSpec((tm, D_MODEL), lambda i, k: (i, 0)),
        out_shape=jax.ShapeDtypeStruct((t, D_MODEL), F32),
        scratch_shapes=[pltpu.VMEM((tm, D_MODEL), BF16)],
        compiler_params=_params("parallel", "arbitrary"),
        name="ffn",
    )(h, gain, w_gu, w_gu, w_down)


def _mixer_proj_kernel(h_ref, g_ref, w_ref, qkg_ref, cos_ref, sin_ref, o_ref, u_ref, *, tm):
    j = pl.program_id(1)

    @pl.when(j == 0)
    def _():
        u_ref[...] = _rms(h_ref[...], g_ref[...]).astype(BF16)

    u = u_ref[...]

    def store(head, y):
        for b in range(tm // ATTN_BLOCK):
            o_ref[head, b] = y[:, b * ATTN_BLOCK:(b + 1) * ATTN_BLOCK].astype(BF16)

    def run(finish):
        starts = list(range(0, MIX_BLOCK, MXU_DIM))
        r = _dot_tn_nt(w_ref[:, starts[0]:starts[0] + MXU_DIM], u)
        for prev, r0 in zip(starts, starts[1:]):
            r_next = _dot_tn_nt(w_ref[:, r0:r0 + MXU_DIM], u)
            finish(r, prev)
            r = r_next
        finish(r, starts[-1])

    def heads_of(r, r0):
        for c in range(MXU_DIM // HEAD_DIM):
            yield r0 // HEAD_DIM + c, r[c * HEAD_DIM:(c + 1) * HEAD_DIM, :]

    is_rotary = jnp.logical_or(j == DF_Q_BLOCK, j == DF_K_BLOCK)

    @pl.when(jnp.logical_not(is_rotary))
    def _():
        scale = jnp.where(j == SB_Q_BLOCK, -QK_SCALE, 1.0).astype(F32)

        def finish(r, r0):
            for head, x in heads_of(r, r0):
                store(head, x * scale)

        run(finish)

    @pl.when(is_rotary)
    def _():
        gain = jnp.concatenate([qkg_ref[...]] * (tm // HEAD_DIM), axis=1)
        gain = gain * jnp.where(j == DF_Q_BLOCK, QK_SCALE, 1.0).astype(F32)
        cos = cos_ref[...]
        sin = sin_ref[...]
        half = ROT_DIM // 2

        def finish(r, r0):
            for head, x in heads_of(r, r0):
                ms = jnp.mean(x * x, axis=0, keepdims=True)
                y = x * lax.rsqrt(ms + EPS) * gain
                y1, y2 = y[:half], y[half:ROT_DIM]
                store(head, jnp.concatenate(
                    [y1 * cos - y2 * sin, y2 * cos + y1 * sin, y[ROT_DIM:]], axis=0))

        run(finish)


def _mixer_proj(h, gain, w_in, qk_gain, cos_t, sin_t, layer, seq, *, tm=1024):
    t = h.shape[0]
    nblocks = OFF_G_A // MIX_BLOCK
    nseq = seq // tm
    return pl.pallas_call(
        functools.partial(_mixer_proj_kernel, tm=tm),
        grid=(t // tm, nblocks),
        in_specs=[
            pl.BlockSpec((tm, D_MODEL), lambda i, j: (i, 0)),
            pl.BlockSpec((None, 1, D_MODEL), lambda i, j: (layer, 0, 0)),
            pl.BlockSpec((None, D_MODEL, MIX_BLOCK), lambda i, j: (layer, 0, j)),
            pl.BlockSpec((None, None, HEAD_DIM, HEAD_DIM),
                         lambda i, j: (jnp.clip(j - DF_Q_BLOCK, 0, 1), layer, 0, 0)),
            pl.BlockSpec((ROT_DIM // 2, tm), lambda i, j: (0, i % nseq)),
            pl.BlockSpec((ROT_DIM // 2, tm), lambda i, j: (0, i % nseq)),
        ],
        out_specs=pl.BlockSpec((HEADS_PER_MIX_BLOCK, tm // ATTN_BLOCK, HEAD_DIM, ATTN_BLOCK),
                               lambda i, j: (j, i, 0, 0)),
        out_shape=jax.ShapeDtypeStruct(
            (nblocks * HEADS_PER_MIX_BLOCK, t // ATTN_BLOCK, HEAD_DIM, ATTN_BLOCK), BF16),
        scratch_shapes=[pltpu.VMEM((tm, D_MODEL), BF16)],
        compiler_params=_params("parallel", "arbitrary"),
        name="mixer_proj",
    )(h, gain, w_in, qk_gain, cos_t, sin_t)


def _sb_kernel(q_ref, k_ref, v_ref, o_ref, z_scr, w_scr, acc_scr):
    tq = tk = ATTN_BLOCK
    ngroups = tk // SUBLANES
    seg = tk // SUBLANES
    group = lambda g: slice(g * SUBLANES, (g + 1) * SUBLANES)

    def logits(qi, kb):
        z_scr[...] = _dot_tn(k_ref[kb], q_ref[qi])

    def weights(masked):
        if masked:
            sub = lax.broadcasted_iota(jnp.int32, (SUBLANES, tq), 0)
            t_pos = _block_position(lax.broadcasted_iota(jnp.int32, (SUBLANES, tq), 1))
        run = jnp.ones((SUBLANES, tq), F32)
        for g in range(ngroups):
            beta = 1.0 / (1.0 + jnp.exp2(z_scr[group(g), :]))
            keep = 1.0 - beta
            if masked:
                mask = sub * seg + (seg - 1 - g) < t_pos
                beta = jnp.where(mask, beta, 0.0)
                keep = jnp.where(mask, keep, 1.0)
            w_scr[group(g), :] = beta * run
            run = run * keep
        return run

    def accumulate(kb, totals, carry):
        offs = [None] * SUBLANES
        for i in reversed(range(SUBLANES)):
            offs[i] = carry
            carry = carry * totals[i:i + 1, :]
        off = jnp.concatenate(offs, axis=0)
        attn = jnp.concatenate([w_scr[group(g), :] * off for g in range(ngroups)],
                               axis=0).astype(BF16)
        acc_scr[...] += _dot(v_ref[kb], attn)
        return carry

    def query_block(qi, _):
        def body(it, state):
            totals, carry = state
            kb = qi - it
            carry = accumulate(kb, totals, carry)
            totals = weights(False)
            logits(qi, jnp.maximum(kb - 2, 0))
            return totals, carry

        acc_scr[...] = jnp.zeros_like(acc_scr)
        logits(qi, qi)
        totals = weights(True)
        logits(qi, jnp.maximum(qi - 1, 0))
        totals, carry = lax.fori_loop(0, qi, body, (totals, jnp.ones((1, tq), F32)))
        accumulate(0, totals, carry)
        o_ref[pl.ds(pl.multiple_of(qi * tq, tq), tq), :] = acc_scr[...].T.astype(BF16)
        return 0

    lax.fori_loop(0, q_ref.shape[0], query_block, 0)


def _sb_attention(qkv, batch, seq):
    nq = seq // ATTN_BLOCK
    t = batch * seq
    head_seq = (None, nq, HEAD_DIM, ATTN_BLOCK)
    first = lambda block: block * HEADS_PER_MIX_BLOCK
    return pl.pallas_call(
        _sb_kernel,
        grid=(batch, SB_HEADS),
        in_specs=[
            pl.BlockSpec(head_seq, lambda b, h: (first(SB_Q_BLOCK) + h, b, 0, 0)),
            pl.BlockSpec(head_seq, lambda b, h: (first(SB_K_BLOCK) + h, b, 0, 0)),
            pl.BlockSpec(head_seq, lambda b, h: (first(SB_V_BLOCK) + h, b, 0, 0)),
        ],
        out_specs=pl.BlockSpec((seq, HEAD_DIM), lambda b, h: (b, h)),
        out_shape=jax.ShapeDtypeStruct((t, SB_WIDTH), BF16),
        scratch_shapes=[pltpu.VMEM((ATTN_BLOCK, ATTN_BLOCK), F32),
                        pltpu.VMEM((ATTN_BLOCK, ATTN_BLOCK), F32),
                        pltpu.VMEM((HEAD_DIM, ATTN_BLOCK), F32)],
        compiler_params=_params("parallel", "parallel"),
        name="sb_attention",
    )(qkv, qkv, qkv)


def _diff_kernel(lq1_ref, lk1_ref, lq2_ref, lk2_ref, sub_ref, q_ref, k_ref, v_ref, o_ref,
                 s_scr, acc_scr, *, lambda_init):
    tq = tk = ATTN_BLOCK

    def logits(qi, kb, masked):
        maxima = []
        for half in range(2):
            s = _dot_tn(k_ref[half, kb], q_ref[half, qi])
            if masked:
                s_pos = _block_position(lax.broadcasted_iota(jnp.int32, (tk, tq), 0))
                t_pos = _block_position(lax.broadcasted_iota(jnp.int32, (tk, tq), 1))
                s = jnp.where(s_pos <= t_pos, s, MASK_VALUE)
            s_scr[half] = s
            maxima.append(jnp.max(s, axis=0, keepdims=True))
        return tuple(maxima)

    def accumulate(kb, maxima, stats):
        v_t = jnp.concatenate([v_ref[0, kb], v_ref[1, kb]], axis=0)
        new_stats = []
        for half in range(2):
            m, l = stats[half]
            m_new = jnp.maximum(m, maxima[half])
            alpha = jnp.exp2(m - m_new)
            p = jnp.exp2(s_scr[half] - m_new)
            l = alpha * l + jnp.sum(p, axis=0, keepdims=True)
            acc_scr[half] = alpha * acc_scr[half] + _dot(v_t, p.astype(BF16))
            new_stats.append((m_new, l))
        return tuple(new_stats)

    lam = (jnp.exp(jnp.sum(lq1_ref[...] * lk1_ref[...], axis=1, keepdims=True))
           - jnp.exp(jnp.sum(lq2_ref[...] * lk2_ref[...], axis=1, keepdims=True))
           + lambda_init)

    def query_block(qi, _):
        def body(it, state):
            maxima, stats = state
            kb = qi - it
            stats = accumulate(kb, maxima, stats)
            return logits(qi, kb - 1, False), stats

        init = (jnp.full((1, tq), MASK_VALUE, F32), jnp.zeros((1, tq), F32))
        acc_scr[...] = jnp.zeros_like(acc_scr)
        maxima, stats = lax.fori_loop(0, qi, body, (logits(qi, qi, True), (init, init)))
        (_, l1), (_, l2) = accumulate(0, maxima, stats)
        y = (acc_scr[0] / l1 - lam * (acc_scr[1] / l2)).T
        o_ref[pl.ds(pl.multiple_of(qi * tq, tq), tq), :] = (
            _rms(y, sub_ref[...]) * (1.0 - lambda_init)).astype(BF16)
        return 0

    lax.fori_loop(0, q_ref.shape[1], query_block, 0)


def _diff_attention(lams, sub_gain, qkv, layer, batch, seq, lambda_init):
    nq = seq // ATTN_BLOCK
    t = batch * seq
    pair_seq = (2, nq, HEAD_DIM, ATTN_BLOCK)
    first_pair = lambda block: block * HEADS_PER_MIX_BLOCK // 2
    lam_spec = pl.BlockSpec((None, 1, HEAD_DIM), lambda b, h: (layer, 0, 0))
    return pl.pallas_call(
        functools.partial(_diff_kernel, lambda_init=lambda_init),
        grid=(batch, DIFF_HEADS),
        in_specs=[
            lam_spec, lam_spec, lam_spec, lam_spec,
            pl.BlockSpec((None, 1, DIFF_V_DIM), lambda b, h: (layer, 0, 0)),
            pl.BlockSpec(pair_seq, lambda b, h: (first_pair(DF_Q_BLOCK) + h, b, 0, 0)),
            pl.BlockSpec(pair_seq, lambda b, h: (first_pair(DF_K_BLOCK) + h, b, 0, 0)),
            pl.BlockSpec(pair_seq, lambda b, h: (first_pair(DF_V_BLOCK) + h, b, 0, 0)),
        ],
        out_specs=pl.BlockSpec((seq, DIFF_V_DIM), lambda b, h: (b, h)),
        out_shape=jax.ShapeDtypeStruct((t, DIFF_V_WIDTH), BF16),
        scratch_shapes=[pltpu.VMEM((2, ATTN_BLOCK, ATTN_BLOCK), F32),
                        pltpu.VMEM((2, DIFF_V_DIM, ATTN_BLOCK), F32)],
        compiler_params=_params("parallel", "parallel"),
        name="diff_attention",
    )(*lams, sub_gain, qkv, qkv, qkv)


def _merge_kernel(h_ref, g_ref, wga_ref, wgb_ref, ya_ref, yb_ref, wa_ref, wb_ref, o_ref, u_ref):
    @pl.when(pl.program_id(1) == 0)
    def _():
        u_ref[...] = _rms(h_ref[...], g_ref[...]).astype(BF16)

    u = u_ref[...]
    merged = (jax.nn.sigmoid(_dot(u, wga_ref[...])) * _dot(ya_ref[...], wa_ref[...])
              + jax.nn.sigmoid(_dot(u, wgb_ref[...])) * _dot(yb_ref[...], wb_ref[...]))
    o_ref[...] = merged.astype(BF16)


def _merge(h, gain, w_in, ya, yb, wa, wb, layer, *, tm=1024, tn=512):
    t = h.shape[0]
    ga0 = OFF_G_A // tn
    gb0 = OFF_G_B // tn
    return pl.pallas_call(
        _merge_kernel,
        grid=(t // tm, D_MODEL // tn),
        in_specs=[
            pl.BlockSpec((tm, D_MODEL), lambda i, j: (i, 0)),
            pl.BlockSpec((None, 1, D_MODEL), lambda i, j: (layer, 0, 0)),
            pl.BlockSpec((None, D_MODEL, tn), lambda i, j: (layer, 0, ga0 + j)),
            pl.BlockSpec((None, D_MODEL, tn), lambda i, j: (layer, 0, gb0 + j)),
            pl.BlockSpec((tm, SB_WIDTH), lambda i, j: (i, 0)),
            pl.BlockSpec((tm, DIFF_V_WIDTH), lambda i, j: (i, 0)),
            pl.BlockSpec((None, SB_WIDTH, tn), lambda i, j: (layer, 0, j)),
            pl.BlockSpec((None, DIFF_V_WIDTH, tn), lambda i, j: (layer, 0, j)),
        ],
        out_specs=pl.BlockSpec((tm, tn), lambda i, j: (i, j)),
        out_shape=jax.ShapeDtypeStruct((t, D_MODEL), BF16),
        scratch_shapes=[pltpu.VMEM((tm, D_MODEL), BF16)],
        compiler_params=_params("parallel", "arbitrary"),
        name="merge",
    )(h, gain, w_in, w_in, ya, yb, wa, wb)


def _out_kernel(h_ref, m_ref, w_ref, o_ref):
    o_ref[...] = h_ref[...] + _dot(m_ref[...], w_ref[...])


def _out_proj(h, merged, w_out, layer, *, tm=1024, tn=1024):
    t = h.shape[0]
    return pl.pallas_call(
        _out_kernel,
        grid=(t // tm, D_MODEL // tn),
        in_specs=[
            pl.BlockSpec((tm, tn), lambda i, j: (i, j)),
            pl.BlockSpec((tm, D_MODEL), lambda i, j: (i, 0)),
            pl.BlockSpec((None, D_MODEL, tn), lambda i, j: (layer, 0, j)),
        ],
        out_specs=pl.BlockSpec((tm, tn), lambda i, j: (i, j)),
        out_shape=jax.ShapeDtypeStruct((t, D_MODEL), F32),
        compiler_params=_params("parallel", "arbitrary"),
        name="out_proj",
    )(h, merged, w_out)


def _ple_kernel(h_ref, g_ref, wg_ref, p_ref, wp_ref, og_ref, o_ref, *, segment_order_out):
    h = h_ref[...]
    u = _rms(h, g_ref[...]).astype(BF16)
    gate = jax.nn.sigmoid(_dot(u, wg_ref[...]))
    p = _flip_row_groups(p_ref[...]).astype(BF16)
    out = h + gate * _rms(_dot(p, wp_ref[...]), og_ref[...])
    o_ref[...] = _flip_row_groups(out) if segment_order_out else out


def _ple(h, gain, w_gate, p, w_proj, out_gain, layer, *, segment_order_out=False):
    t = h.shape[0]
    tm = ATTN_BLOCK
    nt = t // tm
    return pl.pallas_call(
        functools.partial(_ple_kernel, segment_order_out=segment_order_out),
        grid=(nt,),
        in_specs=[
            pl.BlockSpec((tm, D_MODEL), lambda i: (i, 0)),
            pl.BlockSpec((None, 1, D_MODEL), lambda i: (layer, 0, 0)),
            pl.BlockSpec((None, D_MODEL, D_MODEL), lambda i: (layer, 0, 0)),
            pl.BlockSpec((None, tm, PLE_DIM), lambda i: (layer, i, 0)),
            pl.BlockSpec((None, PLE_DIM, D_MODEL), lambda i: (layer, 0, 0)),
            pl.BlockSpec((None, 1, D_MODEL), lambda i: (layer, 0, 0)),
        ],
        out_specs=pl.BlockSpec((tm, D_MODEL), lambda i: (i, 0)),
        out_shape=jax.ShapeDtypeStruct((t, D_MODEL), F32),
        compiler_params=_params("parallel"),
        name="ple",
    )(h, gain, w_gate, p, w_proj, out_gain)


def _to_segment_order(a, axis, reverse_positions=False):
    seg = ATTN_BLOCK // SUBLANES
    shape = a.shape
    a = a.reshape(shape[:axis] + (shape[axis] // ATTN_BLOCK, SUBLANES, seg) + shape[axis + 1:])
    if reverse_positions:
        a = jnp.flip(a, axis + 2)
    return jnp.swapaxes(a, axis + 1, axis + 2).reshape(shape)


def _from_segment_order(a, axis):
    seg = ATTN_BLOCK // SUBLANES
    shape = a.shape
    a = a.reshape(shape[:axis] + (shape[axis] // ATTN_BLOCK, seg, SUBLANES) + shape[axis + 1:])
    return jnp.swapaxes(a, axis + 1, axis + 2).reshape(shape)


def _rope_tables(seq):
    pos = jnp.arange(seq, dtype=F32)
    inv_freq = ROPE_THETA ** (-jnp.arange(0, ROT_DIM, 2, dtype=F32) / ROT_DIM)
    ang = _to_segment_order(pos[:, None] * inv_freq[None, :], 0, reverse_positions=True)
    return jnp.cos(ang).T, jnp.sin(ang).T


def kernel(x, p, ffn1_norm, ffn1_w_gu, ffn1_w_down, mix_norm, w_in, diff_q_norm, diff_k_norm,
           diff_lambda_q1, diff_lambda_k1, diff_lambda_q2, diff_lambda_k2, diff_sub_norm,
           w_branch_a, w_branch_b, w_out, ffn2_norm, ffn2_w_gu, ffn2_w_down, ple_norm,
           ple_w_gate, ple_w_proj, ple_out_norm):
    batch, seq, _ = x.shape
    t = batch * seq
    row = lambda a: a.reshape(DEPTH, 1, a.shape[-1])
    bf = lambda a: a.astype(BF16)

    ffn1_w_gu, ffn1_w_down, ffn2_w_gu, ffn2_w_down = map(
        bf, (ffn1_w_gu, ffn1_w_down, ffn2_w_gu, ffn2_w_down))
    w_in, w_branch_a, w_branch_b, w_out = map(bf, (w_in, w_branch_a, w_branch_b, w_out))
    ple_w_gate, ple_w_proj = bf(ple_w_gate), bf(ple_w_proj)
    ffn1_norm, mix_norm, ffn2_norm, ple_norm, ple_out_norm, diff_sub_norm = map(
        row, (ffn1_norm, mix_norm, ffn2_norm, ple_norm, ple_out_norm, diff_sub_norm))
    lams = tuple(map(row, (diff_lambda_q1, diff_lambda_k1, diff_lambda_q2, diff_lambda_k2)))
    qk_gain = jnp.broadcast_to(jnp.stack([diff_q_norm, diff_k_norm])[..., None],
                               (2, DEPTH, HEAD_DIM, HEAD_DIM))
    cos_t, sin_t = _rope_tables(seq)
    p = _to_segment_order(p.reshape(DEPTH, t, PLE_DIM), 1)

    h = _to_segment_order(x.reshape(t, D_MODEL), 0)
    for layer in range(DEPTH):
        lambda_init = 0.8 - 0.6 * math.exp(-0.3 * layer)
        h = _ffn(h, ffn1_norm, ffn1_w_gu, ffn1_w_down, layer, segment_order_in=(layer == 0))

        qkv = _mixer_proj(h, mix_norm, w_in, qk_gain, cos_t, sin_t, layer, seq)
        y_a = _sb_attention(qkv, batch, seq)
        y_b = _diff_attention(lams, diff_sub_norm, qkv, layer, batch, seq, lambda_init)
        merged = _merge(h, mix_norm, w_in, y_a, y_b, w_branch_a, w_branch_b, layer)
        h = _out_proj(h, merged, w_out, layer)

        h = _ffn(h, ffn2_norm, ffn2_w_gu, ffn2_w_down, layer)
        h = _ple(h, ple_norm, ple_w_gate, p, ple_w_proj, ple_out_norm, layer,
                 segment_order_out=(layer == DEPTH - 1))
    return _from_segment_order(h, 0).reshape(batch, seq, D_MODEL)
```

```python
import functools
import math

import jax
import jax.numpy as jnp
from jax import lax
from jax.experimental import pallas as pl
from jax.experimental.pallas import tpu as pltpu

D_MODEL = 2048
DEPTH = 2
HEAD_DIM = 128
SB_HEADS = 8
DIFF_HEADS = 4
DIFF_V_DIM = 2 * HEAD_DIM
SB_WIDTH = SB_HEADS * HEAD_DIM
DIFF_QK_WIDTH = DIFF_HEADS * 2 * HEAD_DIM
DIFF_V_WIDTH = DIFF_HEADS * DIFF_V_DIM
D_FF = 5632
PLE_DIM = 256
ROPE_THETA = 500000.0
ROT_DIM = HEAD_DIM // 4
EPS = 1e-6
FFN_RES_WEIGHT = 0.5
QK_SCALE = HEAD_DIM ** -0.5 * math.log2(math.e)

OFF_DF_Q = 3 * SB_WIDTH
OFF_DF_V = OFF_DF_Q + 2 * DIFF_QK_WIDTH
OFF_G_A = OFF_DF_V + DIFF_V_WIDTH
OFF_G_B = OFF_G_A + D_MODEL

F32 = jnp.float32
BF16 = jnp.bfloat16

VMEM_LIMIT_BYTES = 56 * 1024 * 1024
SUBLANES = 8
MXU_DIM = 256
ATTN_BLOCK = 512
MASK_VALUE = -1e30

MIX_BLOCK = 1024
SB_Q_BLOCK, SB_K_BLOCK, SB_V_BLOCK, DF_Q_BLOCK, DF_K_BLOCK, DF_V_BLOCK = range(6)
HEADS_PER_MIX_BLOCK = MIX_BLOCK // HEAD_DIM


def _params(*sem):
    return pltpu.CompilerParams(dimension_semantics=sem, vmem_limit_bytes=VMEM_LIMIT_BYTES)


def _rms(x, g):
    ms = jnp.mean(x * x, axis=-1, keepdims=True)
    return x * lax.rsqrt(ms + EPS) * g


def _dot(a, b):
    return jnp.dot(a, b, preferred_element_type=F32)


def _dot_nt(a, b):
    return lax.dot_general(a, b, (((1,), (1,)), ((), ())), preferred_element_type=F32)


def _dot_tn_nt(a, b):
    return lax.dot_general(a, b, (((0,), (1,)), ((), ())), preferred_element_type=F32)


def _dot_tn(a, b):
    return lax.dot_general(a, b, (((0,), (0,)), ((), ())), preferred_element_type=F32)


def _flip_row_groups(x):
    n = x.shape[0] // SUBLANES
    return jnp.concatenate(
        [x[(n - 1 - g) * SUBLANES:(n - g) * SUBLANES] for g in range(n)], axis=0)


def _block_position(idx):
    seg = ATTN_BLOCK // SUBLANES
    return (idx % SUBLANES) * seg + (seg - 1 - idx // SUBLANES)


def _ffn_kernel(h_ref, g_ref, wg_ref, wu_ref, wd_ref, o_ref, u_ref, *, tn, segment_order_in):
    @pl.when(pl.program_id(1) == 0)
    def _():
        h = h_ref[...]
        if segment_order_in:
            h = jnp.concatenate([_flip_row_groups(h[b:b + ATTN_BLOCK])
                                 for b in range(0, h.shape[0], ATTN_BLOCK)], axis=0)
        u_ref[...] = _rms(h, g_ref[...]).astype(BF16)
        o_ref[...] = h

    u = u_ref[...]
    acts = []
    for c in range(0, wg_ref.shape[1], MXU_DIM):
        sub = slice(c, c + MXU_DIM)
        gate = _dot(u, wg_ref[:, sub])
        up = _dot(u, wu_ref[:, sub])
        acts.append((sub, (gate * jax.nn.sigmoid(gate) * up).astype(BF16)))
    for sub, a in acts:
        for n in range(D_MODEL // tn):
            cols = slice(n * tn, (n + 1) * tn)
            o_ref[:, cols] += FFN_RES_WEIGHT * _dot(a, wd_ref[sub, cols])


def _ffn(h, gain, w_gu, w_down, layer, *, segment_order_in=False, tm=1024, tf=512, tn=512):
    t = h.shape[0]
    assert tm % ATTN_BLOCK == 0
    nf = D_FF // tf
    return pl.pallas_call(
        functools.partial(_ffn_kernel, tn=tn, segment_order_in=segment_order_in),
        grid=(t // tm, nf),
        in_specs=[
            pl.BlockSpec((tm, D_MODEL), lambda i, k: (i, 0)),
            pl.BlockSpec((None, 1, D_MODEL), lambda i, k: (layer, 0, 0)),
            pl.BlockSpec((None, D_MODEL, tf), lambda i, k: (layer, 0, k)),
            pl.BlockSpec((None, D_MODEL, tf), lambda i, k: (layer, 0, k + nf)),
            pl.BlockSpec((None, tf, D_MODEL), lambda i, k: (layer, k, 0)),
        ],
        out_specs=pl.BlockSpec((tm, D_MODEL), lambda i, k: (i, 0)),
        out_shape=jax.ShapeDtypeStruct((t, D_MODEL), F32),
        scratch_shapes=[pltpu.VMEM((tm, D_MODEL), BF16)],
        compiler_params=_params("parallel", "arbitrary"),
        name="ffn",
    )(h, gain, w_gu, w_gu, w_down)


def _mixer_proj_kernel(h_ref, g_ref, w_ref, qkg_ref, cos_ref, sin_ref, o_ref, u_ref, *, tm):
    j = pl.program_id(1)

    @pl.when(j == 0)
    def _():
        u_ref[...] = _rms(h_ref[...], g_ref[...]).astype(BF16)

    u = u_ref[...]

    def store(head, y):
        for b in range(tm // ATTN_BLOCK):
            o_ref[head, b] = y[:, b * ATTN_BLOCK:(b + 1) * ATTN_BLOCK].astype(BF16)

    def run(finish):
        starts = list(range(0, MIX_BLOCK, MXU_DIM))
        r = _dot_tn_nt(w_ref[:, starts[0]:starts[0] + MXU_DIM], u)
        for prev, r0 in zip(starts, starts[1:]):
            r_next = _dot_tn_nt(w_ref[:, r0:r0 + MXU_DIM], u)
            finish(r, prev)
            r = r_next
        finish(r, starts[-1])

    def heads_of(r, r0):
        for c in range(MXU_DIM // HEAD_DIM):
            yield r0 // HEAD_DIM + c, r[c * HEAD_DIM:(c + 1) * HEAD_DIM, :]

    is_rotary = jnp.logical_or(j == DF_Q_BLOCK, j == DF_K_BLOCK)

    @pl.when(jnp.logical_not(is_rotary))
    def _():
        scale = jnp.where(j == SB_Q_BLOCK, -QK_SCALE, 1.0).astype(F32)

        def finish(r, r0):
            for head, x in heads_of(r, r0):
                store(head, x * scale)

        run(finish)

    @pl.when(is_rotary)
    def _():
        gain = jnp.concatenate([qkg_ref[...]] * (tm // HEAD_DIM), axis=1)
        gain = gain * jnp.where(j == DF_Q_BLOCK, QK_SCALE, 1.0).astype(F32)
        cos = cos_ref[...]
        sin = sin_ref[...]
        half = ROT_DIM // 2

        def finish(r, r0):
            for head, x in heads_of(r, r0):
                ms = jnp.mean(x * x, axis=0, keepdims=True)
                y = x * lax.rsqrt(ms + EPS) * gain
                y1, y2 = y[:half], y[half:ROT_DIM]
                store(head, jnp.concatenate(
                    [y1 * cos - y2 * sin, y2 * cos + y1 * sin, y[ROT_DIM:]], axis=0))

        run(finish)


def _mixer_proj(h, gain, w_in, qk_gain, cos_t, sin_t, layer, seq, *, tm=1024):
    t = h.shape[0]
    nblocks = OFF_G_A // MIX_BLOCK
    nseq = seq // tm
    return pl.pallas_call(
        functools.partial(_mixer_proj_kernel, tm=tm),
        grid=(t // tm, nblocks),
        in_specs=[
            pl.BlockSpec((tm, D_MODEL), lambda i, j: (i, 0)),
            pl.BlockSpec((None, 1, D_MODEL), lambda i, j: (layer, 0, 0)),
            pl.BlockSpec((None, D_MODEL, MIX_BLOCK), lambda i, j: (layer, 0, j)),
            pl.BlockSpec((None, None, HEAD_DIM, HEAD_DIM),
                         lambda i, j: (jnp.clip(j - DF_Q_BLOCK, 0, 1), layer, 0, 0)),
            pl.BlockSpec((ROT_DIM // 2, tm), lambda i, j: (0, i % nseq)),
            pl.BlockSpec((ROT_DIM // 2, tm), lambda i, j: (0, i % nseq)),
        ],
        out_specs=[pl.BlockSpec((HEADS_PER_MIX_BLOCK, tm // ATTN_BLOCK, HEAD_DIM, ATTN_BLOCK),
                                lambda i, j: (j, i, 0, 0)),
                   pl.BlockSpec((tm, D_MODEL), lambda i, j: (i, 0))],
        out_shape=[jax.ShapeDtypeStruct(
            (nblocks * HEADS_PER_MIX_BLOCK, t // ATTN_BLOCK, HEAD_DIM, ATTN_BLOCK), BF16),
            jax.ShapeDtypeStruct((t, D_MODEL), BF16)],
        compiler_params=_params("parallel", "arbitrary"),
        name="mixer_proj",
    )(h, gain, w_in, qk_gain, cos_t, sin_t)


def _sb_kernel(q_ref, k_ref, v_ref, o_ref, z_scr, w_scr, acc_scr):
    tq = tk = ATTN_BLOCK
    ngroups = tk // SUBLANES
    seg = tk // SUBLANES
    group = lambda g: slice(g * SUBLANES, (g + 1) * SUBLANES)

    def logits(qi, kb):
        z_scr[...] = _dot_tn(k_ref[kb], q_ref[qi])

    def weights(masked):
        if masked:
            sub = lax.broadcasted_iota(jnp.int32, (SUBLANES, tq), 0)
            t_pos = _block_position(lax.broadcasted_iota(jnp.int32, (SUBLANES, tq), 1))
        run = jnp.ones((SUBLANES, tq), F32)
        for g in range(ngroups):
            beta = 1.0 / (1.0 + jnp.exp2(z_scr[group(g), :]))
            keep = 1.0 - beta
            if masked:
                mask = sub * seg + (seg - 1 - g) < t_pos
                beta = jnp.where(mask, beta, 0.0)
                keep = jnp.where(mask, keep, 1.0)
            w_scr[group(g), :] = beta * run
            run = run * keep
        return run

    def accumulate(kb, totals, carry):
        offs = [None] * SUBLANES
        for i in reversed(range(SUBLANES)):
            offs[i] = carry
            carry = carry * totals[i:i + 1, :]
        off = jnp.concatenate(offs, axis=0)
        attn = jnp.concatenate([w_scr[group(g), :] * off for g in range(ngroups)],
                               axis=0).astype(BF16)
        acc_scr[...] += _dot(v_ref[kb], attn)
        return carry

    def query_block(qi, _):
        def body(it, state):
            totals, carry = state
            kb = qi - it
            carry = accumulate(kb, totals, carry)
            totals = weights(False)
            logits(qi, jnp.maximum(kb - 2, 0))
            return totals, carry

        acc_scr[...] = jnp.zeros_like(acc_scr)
        logits(qi, qi)
        totals = weights(True)
        logits(qi, jnp.maximum(qi - 1, 0))
        totals, carry = lax.fori_loop(0, qi, body, (totals, jnp.ones((1, tq), F32)))
        accumulate(0, totals, carry)
        o_ref[pl.ds(pl.multiple_of(qi * tq, tq), tq), :] = acc_scr[...].T.astype(BF16)
        return 0

    lax.fori_loop(0, q_ref.shape[0], query_block, 0)


def _sb_attention(qkv, batch, seq):
    nq = seq // ATTN_BLOCK
    t = batch * seq
    head_seq = (None, nq, HEAD_DIM, ATTN_BLOCK)
    first = lambda block: block * HEADS_PER_MIX_BLOCK
    return pl.pallas_call(
        _sb_kernel,
        grid=(batch, SB_HEADS),
        in_specs=[
            pl.BlockSpec(head_seq, lambda b, h: (first(SB_Q_BLOCK) + h, b, 0, 0)),
            pl.BlockSpec(head_seq, lambda b, h: (first(SB_K_BLOCK) + h, b, 0, 0)),
            pl.BlockSpec(head_seq, lambda b, h: (first(SB_V_BLOCK) + h, b, 0, 0)),
        ],
        out_specs=pl.BlockSpec((seq, HEAD_DIM), lambda b, h: (b, h)),
        out_shape=jax.ShapeDtypeStruct((t, SB_WIDTH), BF16),
        scratch_shapes=[pltpu.VMEM((ATTN_BLOCK, ATTN_BLOCK), F32),
                        pltpu.VMEM((ATTN_BLOCK, ATTN_BLOCK), F32),
                        pltpu.VMEM((HEAD_DIM, ATTN_BLOCK), F32)],
        compiler_params=_params("parallel", "parallel"),
        name="sb_attention",
    )(qkv, qkv, qkv)


def _diff_kernel(lq1_ref, lk1_ref, lq2_ref, lk2_ref, sub_ref, q_ref, k_ref, v_ref, o_ref,
                 s_scr, acc_scr, *, lambda_init):
    tq = tk = ATTN_BLOCK

    def logits(qi, kb, masked):
        maxima = []
        for half in range(2):
            s = _dot_tn(k_ref[half, kb], q_ref[half, qi])
            if masked:
                s_pos = _block_position(lax.broadcasted_iota(jnp.int32, (tk, tq), 0))
                t_pos = _block_position(lax.broadcasted_iota(jnp.int32, (tk, tq), 1))
                s = jnp.where(s_pos <= t_pos, s, MASK_VALUE)
            s_scr[half] = s
            maxima.append(jnp.max(s, axis=0, keepdims=True))
        return tuple(maxima)

    def accumulate(kb, maxima, stats):
        v_t = jnp.concatenate([v_ref[0, kb], v_ref[1, kb]], axis=0)
        new_stats = []
        for half in range(2):
            m, l = stats[half]
            m_new = jnp.maximum(m, maxima[half])
            alpha = jnp.exp2(m - m_new)
            p = jnp.exp2(s_scr[half] - m_new)
            l = alpha * l + jnp.sum(p, axis=0, keepdims=True)
            acc_scr[half] = alpha * acc_scr[half] + _dot(v_t, p.astype(BF16))
            new_stats.append((m_new, l))
        return tuple(new_stats)

    lam = (jnp.exp(jnp.sum(lq1_ref[...] * lk1_ref[...], axis=1, keepdims=True))
           - jnp.exp(jnp.sum(lq2_ref[...] * lk2_ref[...], axis=1, keepdims=True))
           + lambda_init)

    nq = q_ref.shape[1]

    def query_block(qi, maxima):
        def body(it, state):
            maxima, stats = state
            kb = qi - it
            stats = accumulate(kb, maxima, stats)
            return logits(qi, kb - 1, False), stats

        init = (jnp.full((1, tq), MASK_VALUE, F32), jnp.zeros((1, tq), F32))
        maxima, stats = lax.fori_loop(0, qi, body, (maxima, (init, init)))
        (_, l1), (_, l2) = accumulate(0, maxima, stats)
        y = (acc_scr[0] / l1 - lam * (acc_scr[1] / l2)).T
        o_ref[pl.ds(pl.multiple_of(qi * tq, tq), tq), :] = (
            _rms(y, sub_ref[...]) * (1.0 - lambda_init)).astype(BF16)
        acc_scr[...] = jnp.zeros_like(acc_scr)
        nxt = jnp.minimum(qi + 1, nq - 1)
        return logits(nxt, nxt, True)

    acc_scr[...] = jnp.zeros_like(acc_scr)
    lax.fori_loop(0, nq, query_block, logits(0, 0, True))


def _diff_attention(lams, sub_gain, qkv, layer, batch, seq, lambda_init):
    nq = seq // ATTN_BLOCK
    t = batch * seq
    pair_seq = (2, nq, HEAD_DIM, ATTN_BLOCK)
    first_pair = lambda block: block * HEADS_PER_MIX_BLOCK // 2
    lam_spec = pl.BlockSpec((None, 1, HEAD_DIM), lambda b, h: (layer, 0, 0))
    return pl.pallas_call(
        functools.partial(_diff_kernel, lambda_init=lambda_init),
        grid=(batch, DIFF_HEADS),
        in_specs=[
            lam_spec, lam_spec, lam_spec, lam_spec,
            pl.BlockSpec((None, 1, DIFF_V_DIM), lambda b, h: (layer, 0, 0)),
            pl.BlockSpec(pair_seq, lambda b, h: (first_pair(DF_Q_BLOCK) + h, b, 0, 0)),
            pl.BlockSpec(pair_seq, lambda b, h: (first_pair(DF_K_BLOCK) + h, b, 0, 0)),
            pl.BlockSpec(pair_seq, lambda b, h: (first_pair(DF_V_BLOCK) + h, b, 0, 0)),
        ],
        out_specs=pl.BlockSpec((seq, DIFF_V_DIM), lambda b, h: (b, h)),
        out_shape=jax.ShapeDtypeStruct((t, DIFF_V_WIDTH), BF16),
        scratch_shapes=[pltpu.VMEM((2, ATTN_BLOCK, ATTN_BLOCK), F32),
                        pltpu.VMEM((2, DIFF_V_DIM, ATTN_BLOCK), F32)],
        compiler_params=_params("parallel", "parallel"),
        name="diff_attention",
    )(*lams, sub_gain, qkv, qkv, qkv)


def _merge_kernel(u_ref, wga_ref, wgb_ref, ya_ref, yb_ref, wa_ref, wb_ref, o_ref):
    u = u_ref[...]
    merged = (jax.nn.sigmoid(_dot(u, wga_ref[...])) * _dot(ya_ref[...], wa_ref[...])
              + jax.nn.sigmoid(_dot(u, wgb_ref[...])) * _dot(yb_ref[...], wb_ref[...]))
    o_ref[...] = merged.astype(BF16)


def _merge(u, w_in, ya, yb, wa, wb, layer, *, tm=1024, tn=512):
    t = u.shape[0]
    ga0 = OFF_G_A // tn
    gb0 = OFF_G_B // tn
    return pl.pallas_call(
        _merge_kernel,
        grid=(t // tm, D_MODEL // tn),
        in_specs=[
            pl.BlockSpec((tm, D_MODEL), lambda i, j: (i, 0)),
            pl.BlockSpec((None, D_MODEL, tn), lambda i, j: (layer, 0, ga0 + j)),
            pl.BlockSpec((None, D_MODEL, tn), lambda i, j: (layer, 0, gb0 + j)),
            pl.BlockSpec((tm, SB_WIDTH), lambda i, j: (i, 0)),
            pl.BlockSpec((tm, DIFF_V_WIDTH), lambda i, j: (i, 0)),
            pl.BlockSpec((None, SB_WIDTH, tn), lambda i, j: (layer, 0, j)),
            pl.BlockSpec((None, DIFF_V_WIDTH, tn), lambda i, j: (layer, 0, j)),
        ],
        out_specs=pl.BlockSpec((tm, tn), lambda i, j: (i, j)),
        out_shape=jax.ShapeDtypeStruct((t, D_MODEL), BF16),
        compiler_params=_params("parallel", "arbitrary"),
        name="merge",
    )(u, w_in, w_in, ya, yb, wa, wb)


def _out_kernel(h_ref, m_ref, w_ref, o_ref):
    o_ref[...] = h_ref[...] + _dot(m_ref[...], w_ref[...])


def _out_proj(h, merged, w_out, layer, *, tm=1024, tn=1024):
    t = h.shape[0]
    return pl.pallas_call(
        _out_kernel,
        grid=(t // tm, D_MODEL // tn),
        in_specs=[
            pl.BlockSpec((tm, tn), lambda i, j: (i, j)),
            pl.BlockSpec((tm, D_MODEL), lambda i, j: (i, 0)),
            pl.BlockSpec((None, D_MODEL, tn), lambda i, j: (layer, 0, j)),
        ],
        out_specs=pl.BlockSpec((tm, tn), lambda i, j: (i, j)),
        out_shape=jax.ShapeDtypeStruct((t, D_MODEL), F32),
        compiler_params=_params("parallel", "arbitrary"),
        name="out_proj",
    )(h, merged, w_out)


def _ple_kernel(h_ref, g_ref, wg_ref, p_ref, wp_ref, og_ref, o_ref, *, segment_order_out):
    h = h_ref[...]
    u = _rms(h, g_ref[...]).astype(BF16)
    gate = jax.nn.sigmoid(_dot(u, wg_ref[...]))
    p = _flip_row_groups(p_ref[...]).astype(BF16)
    out = h + gate * _rms(_dot(p, wp_ref[...]), og_ref[...])
    o_ref[...] = _flip_row_groups(out) if segment_order_out else out


def _ple(h, gain, w_gate, p, w_proj, out_gain, layer, *, segment_order_out=False):
    t = h.shape[0]
    tm = ATTN_BLOCK
    nt = t // tm
    return pl.pallas_call(
        functools.partial(_ple_kernel, segment_order_out=segment_order_out),
        grid=(nt,),
        in_specs=[
            pl.BlockSpec((tm, D_MODEL), lambda i: (i, 0)),
            pl.BlockSpec((None, 1, D_MODEL), lambda i: (layer, 0, 0)),
            pl.BlockSpec((None, D_MODEL, D_MODEL), lambda i: (layer, 0, 0)),
            pl.BlockSpec((None, tm, PLE_DIM), lambda i: (layer, i, 0)),
            pl.BlockSpec((None, PLE_DIM, D_MODEL), lambda i: (layer, 0, 0)),
            pl.BlockSpec((None, 1, D_MODEL), lambda i: (layer, 0, 0)),
        ],
        out_specs=pl.BlockSpec((tm, D_MODEL), lambda i: (i, 0)),
        out_shape=jax.ShapeDtypeStruct((t, D_MODEL), F32),
        compiler_params=_params("parallel"),
        name="ple",
    )(h, gain, w_gate, p, w_proj, out_gain)


def _to_segment_order(a, axis, reverse_positions=False):
    seg = ATTN_BLOCK // SUBLANES
    shape = a.shape
    a = a.reshape(shape[:axis] + (shape[axis] // ATTN_BLOCK, SUBLANES, seg) + shape[axis + 1:])
    if reverse_positions:
        a = jnp.flip(a, axis + 2)
    return jnp.swapaxes(a, axis + 1, axis + 2).reshape(shape)


def _from_segment_order(a, axis):
    seg = ATTN_BLOCK // SUBLANES
    shape = a.shape
    a = a.reshape(shape[:axis] + (shape[axis] // ATTN_BLOCK, seg, SUBLANES) + shape[axis + 1:])
    return jnp.swapaxes(a, axis + 1, axis + 2).reshape(shape)


def _rope_tables(seq):
    pos = jnp.arange(seq, dtype=F32)
    inv_freq = ROPE_THETA ** (-jnp.arange(0, ROT_DIM, 2, dtype=F32) / ROT_DIM)
    ang = _to_segment_order(pos[:, None] * inv_freq[None, :], 0, reverse_positions=True)
    return jnp.cos(ang).T, jnp.sin(ang).T


def kernel(x, p, ffn1_norm, ffn1_w_gu, ffn1_w_down, mix_norm, w_in, diff_q_norm, diff_k_norm,
           diff_lambda_q1, diff_lambda_k1, diff_lambda_q2, diff_lambda_k2, diff_sub_norm,
           w_branch_a, w_branch_b, w_out, ffn2_norm, ffn2_w_gu, ffn2_w_down, ple_norm,
           ple_w_gate, ple_w_proj, ple_out_norm):
    batch, seq, _ = x.shape
    t = batch * seq
    row = lambda a: a.reshape(DEPTH, 1, a.shape[-1])
    bf = lambda a: a.astype(BF16)

    ffn1_w_gu, ffn1_w_down, ffn2_w_gu, ffn2_w_down = map(
        bf, (ffn1_w_gu, ffn1_w_down, ffn2_w_gu, ffn2_w_down))
    w_in, w_branch_a, w_branch_b, w_out = map(bf, (w_in, w_branch_a, w_branch_b, w_out))
    ple_w_gate, ple_w_proj = bf(ple_w_gate), bf(ple_w_proj)
    ffn1_norm, mix_norm, ffn2_norm, ple_norm, ple_out_norm, diff_sub_norm = map(
        row, (ffn1_norm, mix_norm, ffn2_norm, ple_norm, ple_out_norm, diff_sub_norm))
    lams = tuple(map(row, (diff_lambda_q1, diff_lambda_k1, diff_lambda_q2, diff_lambda_k2)))
    qk_gain = jnp.broadcast_to(jnp.stack([diff_q_norm, diff_k_norm])[..., None],
                               (2, DEPTH, HEAD_DIM, HEAD_DIM))
    cos_t, sin_t = _rope_tables(seq)
    p = _to_segment_order(p.reshape(DEPTH, t, PLE_DIM), 1)

    h = _to_segment_order(x.reshape(t, D_MODEL), 0)
    for layer in range(DEPTH):
        lambda_init = 0.8 - 0.6 * math.exp(-0.3 * layer)
        h = _ffn(h, ffn1_norm, ffn1_w_gu, ffn1_w_down, layer, segment_order_in=(layer == 0))

        qkv, u = _mixer_proj(h, mix_norm, w_in, qk_gain, cos_t, sin_t, layer, seq)
        y_a = _sb_attention(qkv, batch, seq)
        y_b = _diff_attention(lams, diff_sub_norm, qkv, layer, batch, seq, lambda_init)
        merged = _merge(u, w_in, y_a, y_b, w_branch_a, w_branch_b, layer)
        h = _out_proj(h, merged, w_out, layer)

        h = _ffn(h, ffn2_norm, ffn2_w_gu, ffn2_w_down, layer)
        h = _ple(h, ple_norm, ple_w_gate, p, ple_w_proj, ple_out_norm, layer,
                 segment_order_out=(layer == DEPTH - 1))
    return _from_segment_order(h, 0).reshape(batch, seq, D_MODEL)
```

```python
import functools
import math

import jax
import jax.numpy as jnp
from jax import lax
from jax.experimental import pallas as pl
from jax.experimental.pallas import tpu as pltpu

D_MODEL = 2048
DEPTH = 2
HEAD_DIM = 128
SB_HEADS = 8
DIFF_HEADS = 4
DIFF_V_DIM = 2 * HEAD_DIM
SB_WIDTH = SB_HEADS * HEAD_DIM
DIFF_QK_WIDTH = DIFF_HEADS * 2 * HEAD_DIM
DIFF_V_WIDTH = DIFF_HEADS * DIFF_V_DIM
D_FF = 5632
PLE_DIM = 256
ROPE_THETA = 500000.0
ROT_DIM = HEAD_DIM // 4
EPS = 1e-6
FFN_RES_WEIGHT = 0.5
QK_SCALE = HEAD_DIM ** -0.5 * math.log2(math.e)
SB_Q_SCALE = -0.5 * HEAD_DIM ** -0.5

OFF_DF_Q = 3 * SB_WIDTH
OFF_DF_V = OFF_DF_Q + 2 * DIFF_QK_WIDTH
OFF_G_A = OFF_DF_V + DIFF_V_WIDTH
OFF_G_B = OFF_G_A + D_MODEL

F32 = jnp.float32
BF16 = jnp.bfloat16

VMEM_LIMIT_BYTES = 56 * 1024 * 1024
SUBLANES = 8
MXU_DIM = 256
ATTN_BLOCK = 512
MASK_VALUE = -1e30

MIX_BLOCK = 1024
SB_Q_BLOCK, SB_K_BLOCK, SB_V_BLOCK, DF_Q_BLOCK, DF_K_BLOCK, DF_V_BLOCK = range(6)
HEADS_PER_MIX_BLOCK = MIX_BLOCK // HEAD_DIM


def _params(*sem):
    return pltpu.CompilerParams(dimension_semantics=sem, vmem_limit_bytes=VMEM_LIMIT_BYTES)


def _rms(x, g):
    ms = jnp.mean(x * x, axis=-1, keepdims=True)
    return x * lax.rsqrt(ms + EPS) * g


def _dot(a, b):
    return jnp.dot(a, b, preferred_element_type=F32)


def _dot_nt(a, b):
    return lax.dot_general(a, b, (((1,), (1,)), ((), ())), preferred_element_type=F32)


def _dot_tn_nt(a, b):
    return lax.dot_general(a, b, (((0,), (1,)), ((), ())), preferred_element_type=F32)


def _dot_tn(a, b):
    return lax.dot_general(a, b, (((0,), (0,)), ((), ())), preferred_element_type=F32)


def _flip_row_groups(x):
    n = x.shape[0] // SUBLANES
    return jnp.concatenate(
        [x[(n - 1 - g) * SUBLANES:(n - g) * SUBLANES] for g in range(n)], axis=0)


def _block_position(idx):
    seg = ATTN_BLOCK // SUBLANES
    return (idx % SUBLANES) * seg + (seg - 1 - idx // SUBLANES)


def _ffn_kernel(h_ref, g_ref, wg_ref, wu_ref, wd_ref, o_ref, u_ref, *, tn, segment_order_in):
    @pl.when(pl.program_id(1) == 0)
    def _():
        h = h_ref[...]
        if segment_order_in:
            h = jnp.concatenate([_flip_row_groups(h[b:b + ATTN_BLOCK])
                                 for b in range(0, h.shape[0], ATTN_BLOCK)], axis=0)
        u_ref[...] = _rms(h, g_ref[...]).astype(BF16)
        o_ref[...] = h

    u = u_ref[...]
    acts = []
    for c in range(0, wg_ref.shape[1], MXU_DIM):
        sub = slice(c, c + MXU_DIM)
        gate = _dot(u, wg_ref[:, sub])
        up = _dot(u, wu_ref[:, sub])
        acts.append((sub, (gate * jax.nn.sigmoid(gate) * up).astype(BF16)))
    for sub, a in acts:
        for n in range(D_MODEL // tn):
            cols = slice(n * tn, (n + 1) * tn)
            o_ref[:, cols] += FFN_RES_WEIGHT * _dot(a, wd_ref[sub, cols])


def _ffn(h, gain, w_gu, w_down, layer, *, segment_order_in=False, tm=1024, tf=512, tn=512):
    t = h.shape[0]
    assert tm % ATTN_BLOCK == 0
    nf = D_FF // tf
    return pl.pallas_call(
        functools.partial(_ffn_kernel, tn=tn, segment_order_in=segment_order_in),
        grid=(t // tm, nf),
        in_specs=[
            pl.BlockSpec((tm, D_MODEL), lambda i, k: (i, 0)),
            pl.BlockSpec((None, 1, D_MODEL), lambda i, k: (layer, 0, 0)),
            pl.BlockSpec((None, D_MODEL, tf), lambda i, k: (layer, 0, k)),
            pl.BlockSpec((None, D_MODEL, tf), lambda i, k: (layer, 0, k + nf)),
            pl.BlockSpec((None, tf, D_MODEL), lambda i, k: (layer, k, 0)),
        ],
        out_specs=pl.BlockSpec((tm, D_MODEL), lambda i, k: (i, 0)),
        out_shape=jax.ShapeDtypeStruct((t, D_MODEL), F32),
        scratch_shapes=[pltpu.VMEM((tm, D_MODEL), BF16)],
        compiler_params=_params("parallel", "arbitrary"),
        name="ffn",
    )(h, gain, w_gu, w_gu, w_down)


def _mixer_proj_kernel(h_ref, g_ref, w_ref, qkg_ref, cos_ref, sin_ref, o_ref, u_ref, *, tm):
    j = pl.program_id(1)

    @pl.when(j == 0)
    def _():
        u_ref[...] = _rms(h_ref[...], g_ref[...]).astype(BF16)

    u = u_ref[...]

    def store(head, y):
        for b in range(tm // ATTN_BLOCK):
            o_ref[head, b] = y[:, b * ATTN_BLOCK:(b + 1) * ATTN_BLOCK].astype(BF16)

    def run(finish):
        starts = list(range(0, MIX_BLOCK, MXU_DIM))
        r = _dot_tn_nt(w_ref[:, starts[0]:starts[0] + MXU_DIM], u)
        for prev, r0 in zip(starts, starts[1:]):
            r_next = _dot_tn_nt(w_ref[:, r0:r0 + MXU_DIM], u)
            finish(r, prev)
            r = r_next
        finish(r, starts[-1])

    def heads_of(r, r0):
        for c in range(MXU_DIM // HEAD_DIM):
            yield r0 // HEAD_DIM + c, r[c * HEAD_DIM:(c + 1) * HEAD_DIM, :]

    is_rotary = jnp.logical_or(j == DF_Q_BLOCK, j == DF_K_BLOCK)

    @pl.when(jnp.logical_not(is_rotary))
    def _():
        scale = jnp.where(j == SB_Q_BLOCK, SB_Q_SCALE, 1.0).astype(F32)

        def finish(r, r0):
            for head, x in heads_of(r, r0):
                store(head, x * scale)

        run(finish)

    @pl.when(is_rotary)
    def _():
        gain = jnp.concatenate([qkg_ref[...]] * (tm // HEAD_DIM), axis=1)
        gain = gain * jnp.where(j == DF_Q_BLOCK, QK_SCALE, 1.0).astype(F32)
        cos = cos_ref[...]
        sin = sin_ref[...]
        half = ROT_DIM // 2

        def finish(r, r0):
            for head, x in heads_of(r, r0):
                ms = jnp.mean(x * x, axis=0, keepdims=True)
                y = x * lax.rsqrt(ms + EPS) * gain
                y1, y2 = y[:half], y[half:ROT_DIM]
                store(head, jnp.concatenate(
                    [y1 * cos - y2 * sin, y2 * cos + y1 * sin, y[ROT_DIM:]], axis=0))

        run(finish)


def _mixer_proj(h, gain, w_in, qk_gain, cos_t, sin_t, layer, seq, *, tm=1024):
    t = h.shape[0]
    nblocks = OFF_G_A // MIX_BLOCK
    nseq = seq // tm
    return pl.pallas_call(
        functools.partial(_mixer_proj_kernel, tm=tm),
        grid=(t // tm, nblocks),
        in_specs=[
            pl.BlockSpec((tm, D_MODEL), lambda i, j: (i, 0)),
            pl.BlockSpec((None, 1, D_MODEL), lambda i, j: (layer, 0, 0)),
            pl.BlockSpec((None, D_MODEL, MIX_BLOCK), lambda i, j: (layer, 0, j)),
            pl.BlockSpec((None, None, HEAD_DIM, HEAD_DIM),
                         lambda i, j: (jnp.clip(j - DF_Q_BLOCK, 0, 1), layer, 0, 0)),
            pl.BlockSpec((ROT_DIM // 2, tm), lambda i, j: (0, i % nseq)),
            pl.BlockSpec((ROT_DIM // 2, tm), lambda i, j: (0, i % nseq)),
        ],
        out_specs=[pl.BlockSpec((HEADS_PER_MIX_BLOCK, tm // ATTN_BLOCK, HEAD_DIM, ATTN_BLOCK),
                                lambda i, j: (j, i, 0, 0)),
                   pl.BlockSpec((tm, D_MODEL), lambda i, j: (i, 0))],
        out_shape=[jax.ShapeDtypeStruct(
            (nblocks * HEADS_PER_MIX_BLOCK, t // ATTN_BLOCK, HEAD_DIM, ATTN_BLOCK), BF16),
            jax.ShapeDtypeStruct((t, D_MODEL), BF16)],
        compiler_params=_params("parallel", "arbitrary"),
        name="mixer_proj",
    )(h, gain, w_in, qk_gain, cos_t, sin_t)


def _sb_kernel(q_ref, k_ref, v_ref, o_ref, z_scr, w_scr, acc_scr):
    tq = tk = ATTN_BLOCK
    ngroups = tk // SUBLANES
    seg = tk // SUBLANES
    group = lambda g: slice(g * SUBLANES, (g + 1) * SUBLANES)

    def logits(qi, kb):
        z_scr[...] = _dot_tn(k_ref[kb], q_ref[qi])

    def weights(masked):
        if masked:
            sub = lax.broadcasted_iota(jnp.int32, (SUBLANES, tq), 0)
            t_pos = _block_position(lax.broadcasted_iota(jnp.int32, (SUBLANES, tq), 1))
        run = jnp.ones((SUBLANES, tq), F32)
        for g in range(ngroups):
            th = 0.5 * jnp.tanh(z_scr[group(g), :])
            if masked:
                th = jnp.where(sub * seg + (seg - 1 - g) < t_pos, th, 0.5)
            beta = 0.5 - th
            keep = 0.5 + th
            w_scr[group(g), :] = beta * run
            run = run * keep
        return run

    def accumulate(kb, totals, carry):
        offs = [None] * SUBLANES
        for i in reversed(range(SUBLANES)):
            offs[i] = carry
            carry = carry * totals[i:i + 1, :]
        off = jnp.concatenate(offs, axis=0)
        attn = jnp.concatenate([w_scr[group(g), :] * off for g in range(ngroups)],
                               axis=0).astype(BF16)
        acc_scr[...] += _dot(v_ref[kb], attn)
        return carry

    def query_block(qi, _):
        def body(it, state):
            totals, carry = state
            kb = qi - it
            carry = accumulate(kb, totals, carry)
            totals = weights(False)
            logits(qi, jnp.maximum(kb - 2, 0))
            return totals, carry

        acc_scr[...] = jnp.zeros_like(acc_scr)
        logits(qi, qi)
        totals = weights(True)
        logits(qi, jnp.maximum(qi - 1, 0))
        totals, carry = lax.fori_loop(0, qi, body, (totals, jnp.ones((1, tq), F32)))
        accumulate(0, totals, carry)
        o_ref[pl.ds(pl.multiple_of(qi * tq, tq), tq), :] = acc_scr[...].T.astype(BF16)
        return 0

    lax.fori_loop(0, q_ref.shape[0], query_block, 0)


def _sb_attention(qkv, batch, seq):
    nq = seq // ATTN_BLOCK
    t = batch * seq
    head_seq = (None, nq, HEAD_DIM, ATTN_BLOCK)
    first = lambda block: block * HEADS_PER_MIX_BLOCK
    return pl.pallas_call(
        _sb_kernel,
        grid=(batch, SB_HEADS),
        in_specs=[
            pl.BlockSpec(head_seq, lambda b, h: (first(SB_Q_BLOCK) + h, b, 0, 0)),
            pl.BlockSpec(head_seq, lambda b, h: (first(SB_K_BLOCK) + h, b, 0, 0)),
            pl.BlockSpec(head_seq, lambda b, h: (first(SB_V_BLOCK) + h, b, 0, 0)),
        ],
        out_specs=pl.BlockSpec((seq, HEAD_DIM), lambda b, h: (b, h)),
        out_shape=jax.ShapeDtypeStruct((t, SB_WIDTH), BF16),
        scratch_shapes=[pltpu.VMEM((ATTN_BLOCK, ATTN_BLOCK), F32),
                        pltpu.VMEM((ATTN_BLOCK, ATTN_BLOCK), F32),
                        pltpu.VMEM((HEAD_DIM, ATTN_BLOCK), F32)],
        compiler_params=_params("parallel", "parallel"),
        name="sb_attention",
    )(qkv, qkv, qkv)


def _diff_kernel(lq1_ref, lk1_ref, lq2_ref, lk2_ref, sub_ref, q_ref, k_ref, v_ref, o_ref,
                 s_scr, acc_scr, *, lambda_init):
    tq = tk = ATTN_BLOCK

    def logits(qi, kb, masked):
        maxima = []
        for half in range(2):
            s = _dot_tn(k_ref[half, kb], q_ref[half, qi])
            if masked:
                s_pos = _block_position(lax.broadcasted_iota(jnp.int32, (tk, tq), 0))
                t_pos = _block_position(lax.broadcasted_iota(jnp.int32, (tk, tq), 1))
                s = jnp.where(s_pos <= t_pos, s, MASK_VALUE)
            s_scr[half] = s
            maxima.append(jnp.max(s, axis=0, keepdims=True))
        return tuple(maxima)

    def accumulate(kb, maxima, stats):
        v_t = jnp.concatenate([v_ref[0, kb], v_ref[1, kb]], axis=0)
        new_stats = []
        for half in range(2):
            m, l = stats[half]
            m_new = jnp.maximum(m, maxima[half])
            alpha = jnp.exp2(m - m_new)
            p = jnp.exp2(s_scr[half] - m_new)
            l = alpha * l + jnp.sum(p, axis=0, keepdims=True)
            acc_scr[half] = alpha * acc_scr[half] + _dot(v_t, p.astype(BF16))
            new_stats.append((m_new, l))
        return tuple(new_stats)

    lam = (jnp.exp(jnp.sum(lq1_ref[...] * lk1_ref[...], axis=1, keepdims=True))
           - jnp.exp(jnp.sum(lq2_ref[...] * lk2_ref[...], axis=1, keepdims=True))
           + lambda_init)

    nq = q_ref.shape[1]

    def query_block(qi, maxima):
        def body(it, state):
            maxima, stats = state
            kb = qi - it
            stats = accumulate(kb, maxima, stats)
            return logits(qi, kb - 1, False), stats

        init = (jnp.full((1, tq), MASK_VALUE, F32), jnp.zeros((1, tq), F32))
        maxima, stats = lax.fori_loop(0, qi, body, (maxima, (init, init)))
        (_, l1), (_, l2) = accumulate(0, maxima, stats)
        y = (acc_scr[0] / l1 - lam * (acc_scr[1] / l2)).T
        o_ref[pl.ds(pl.multiple_of(qi * tq, tq), tq), :] = (
            _rms(y, sub_ref[...]) * (1.0 - lambda_init)).astype(BF16)
        acc_scr[...] = jnp.zeros_like(acc_scr)
        nxt = jnp.minimum(qi + 1, nq - 1)
        return logits(nxt, nxt, True)

    acc_scr[...] = jnp.zeros_like(acc_scr)
    lax.fori_loop(0, nq, query_block, logits(0, 0, True))


def _diff_attention(lams, sub_gain, qkv, layer, batch, seq, lambda_init):
    nq = seq // ATTN_BLOCK
    t = batch * seq
    pair_seq = (2, nq, HEAD_DIM, ATTN_BLOCK)
    first_pair = lambda block: block * HEADS_PER_MIX_BLOCK // 2
    lam_spec = pl.BlockSpec((None, 1, HEAD_DIM), lambda b, h: (layer, 0, 0))
    return pl.pallas_call(
        functools.partial(_diff_kernel, lambda_init=lambda_init),
        grid=(batch, DIFF_HEADS),
        in_specs=[
            lam_spec, lam_spec, lam_spec, lam_spec,
            pl.BlockSpec((None, 1, DIFF_V_DIM), lambda b, h: (layer, 0, 0)),
            pl.BlockSpec(pair_seq, lambda b, h: (first_pair(DF_Q_BLOCK) + h, b, 0, 0)),
            pl.BlockSpec(pair_seq, lambda b, h: (first_pair(DF_K_BLOCK) + h, b, 0, 0)),
            pl.BlockSpec(pair_seq, lambda b, h: (first_pair(DF_V_BLOCK) + h, b, 0, 0)),
        ],
        out_specs=pl.BlockSpec((seq, DIFF_V_DIM), lambda b, h: (b, h)),
        out_shape=jax.ShapeDtypeStruct((t, DIFF_V_WIDTH), BF16),
        scratch_shapes=[pltpu.VMEM((2, ATTN_BLOCK, ATTN_BLOCK), F32),
                        pltpu.VMEM((2, DIFF_V_DIM, ATTN_BLOCK), F32)],
        compiler_params=_params("parallel", "parallel"),
        name="diff_attention",
    )(*lams, sub_gain, qkv, qkv, qkv)


def _merge_kernel(u_ref, wga_ref, wgb_ref, ya_ref, yb_ref, wa_ref, wb_ref, o_ref):
    u = u_ref[...]
    merged = (jax.nn.sigmoid(_dot(u, wga_ref[...])) * _dot(ya_ref[...], wa_ref[...])
              + jax.nn.sigmoid(_dot(u, wgb_ref[...])) * _dot(yb_ref[...], wb_ref[...]))
    o_ref[...] = merged.astype(BF16)


def _merge(u, w_in, ya, yb, wa, wb, layer, *, tm=1024, tn=512):
    t = u.shape[0]
    ga0 = OFF_G_A // tn
    gb0 = OFF_G_B // tn
    return pl.pallas_call(
        _merge_kernel,
        grid=(t // tm, D_MODEL // tn),
        in_specs=[
            pl.BlockSpec((tm, D_MODEL), lambda i, j: (i, 0)),
            pl.BlockSpec((None, D_MODEL, tn), lambda i, j: (layer, 0, ga0 + j)),
            pl.BlockSpec((None, D_MODEL, tn), lambda i, j: (layer, 0, gb0 + j)),
            pl.BlockSpec((tm, SB_WIDTH), lambda i, j: (i, 0)),
            pl.BlockSpec((tm, DIFF_V_WIDTH), lambda i, j: (i, 0)),
            pl.BlockSpec((None, SB_WIDTH, tn), lambda i, j: (layer, 0, j)),
            pl.BlockSpec((None, DIFF_V_WIDTH, tn), lambda i, j: (layer, 0, j)),
        ],
        out_specs=pl.BlockSpec((tm, tn), lambda i, j: (i, j)),
        out_shape=jax.ShapeDtypeStruct((t, D_MODEL), BF16),
        compiler_params=_params("parallel", "arbitrary"),
        name="merge",
    )(u, w_in, w_in, ya, yb, wa, wb)


def _out_kernel(h_ref, m_ref, w_ref, o_ref):
    o_ref[...] = h_ref[...] + _dot(m_ref[...], w_ref[...])


def _out_proj(h, merged, w_out, layer, *, tm=1024, tn=1024):
    t = h.shape[0]
    return pl.pallas_call(
        _out_kernel,
        grid=(t // tm, D_MODEL // tn),
        in_specs=[
            pl.BlockSpec((tm, tn), lambda i, j: (i, j)),
            pl.BlockSpec((tm, D_MODEL), lambda i, j: (i, 0)),
            pl.BlockSpec((None, D_MODEL, tn), lambda i, j: (layer, 0, j)),
        ],
        out_specs=pl.BlockSpec((tm, tn), lambda i, j: (i, j)),
        out_shape=jax.ShapeDtypeStruct((t, D_MODEL), F32),
        compiler_params=_params("parallel", "arbitrary"),
        name="out_proj",
    )(h, merged, w_out)


def _ple_kernel(h_ref, g_ref, wg_ref, p_ref, wp_ref, og_ref, o_ref, *, segment_order_out):
    h = h_ref[...]
    u = _rms(h, g_ref[...]).astype(BF16)
    gate = jax.nn.sigmoid(_dot(u, wg_ref[...]))
    p = _flip_row_groups(p_ref[...]).astype(BF16)
    out = h + gate * _rms(_dot(p, wp_ref[...]), og_ref[...])
    o_ref[...] = _flip_row_groups(out) if segment_order_out else out


def _ple(h, gain, w_gate, p, w_proj, out_gain, layer, *, segment_order_out=False):
    t = h.shape[0]
    tm = ATTN_BLOCK
    nt = t // tm
    return pl.pallas_call(
        functools.partial(_ple_kernel, segment_order_out=segment_order_out),
        grid=(nt,),
        in_specs=[
            pl.BlockSpec((tm, D_MODEL), lambda i: (i, 0)),
            pl.BlockSpec((None, 1, D_MODEL), lambda i: (layer, 0, 0)),
            pl.BlockSpec((None, D_MODEL, D_MODEL), lambda i: (layer, 0, 0)),
            pl.BlockSpec((None, tm, PLE_DIM), lambda i: (layer, i, 0)),
            pl.BlockSpec((None, PLE_DIM, D_MODEL), lambda i: (layer, 0, 0)),
            pl.BlockSpec((None, 1, D_MODEL), lambda i: (layer, 0, 0)),
        ],
        out_specs=pl.BlockSpec((tm, D_MODEL), lambda i: (i, 0)),
        out_shape=jax.ShapeDtypeStruct((t, D_MODEL), F32),
        compiler_params=_params("parallel"),
        name="ple",
    )(h, gain, w_gate, p, w_proj, out_gain)


def _to_segment_order(a, axis, reverse_positions=False):
    seg = ATTN_BLOCK // SUBLANES
    shape = a.shape
    a = a.reshape(shape[:axis] + (shape[axis] // ATTN_BLOCK, SUBLANES, seg) + shape[axis + 1:])
    if reverse_positions:
        a = jnp.flip(a, axis + 2)
    return jnp.swapaxes(a, axis + 1, axis + 2).reshape(shape)


def _from_segment_order(a, axis):
    seg = ATTN_BLOCK // SUBLANES
    shape = a.shape
    a = a.reshape(shape[:axis] + (shape[axis] // ATTN_BLOCK, seg, SUBLANES) + shape[axis + 1:])
    return jnp.swapaxes(a, axis + 1, axis + 2).reshape(shape)


def _rope_tables(seq):
    pos = jnp.arange(seq, dtype=F32)
    inv_freq = ROPE_THETA ** (-jnp.arange(0, ROT_DIM, 2, dtype=F32) / ROT_DIM)
    ang = _to_segment_order(pos[:, None] * inv_freq[None, :], 0, reverse_positions=True)
    return jnp.cos(ang).T, jnp.sin(ang).T


def kernel(x, p, ffn1_norm, ffn1_w_gu, ffn1_w_down, mix_norm, w_in, diff_q_norm, diff_k_norm,
           diff_lambda_q1, diff_lambda_k1, diff_lambda_q2, diff_lambda_k2, diff_sub_norm,
           w_branch_a, w_branch_b, w_out, ffn2_norm, ffn2_w_gu, ffn2_w_down, ple_norm,
           ple_w_gate, ple_w_proj, ple_out_norm):
    batch, seq, _ = x.shape
    t = batch * seq
    row = lambda a: a.reshape(DEPTH, 1, a.shape[-1])
    bf = lambda a: a.astype(BF16)

    ffn1_w_gu, ffn1_w_down, ffn2_w_gu, ffn2_w_down = map(
        bf, (ffn1_w_gu, ffn1_w_down, ffn2_w_gu, ffn2_w_down))
    w_in, w_branch_a, w_branch_b, w_out = map(bf, (w_in, w_branch_a, w_branch_b, w_out))
    ple_w_gate, ple_w_proj = bf(ple_w_gate), bf(ple_w_proj)
    ffn1_norm, mix_norm, ffn2_norm, ple_norm, ple_out_norm, diff_sub_norm = map(
        row, (ffn1_norm, mix_norm, ffn2_norm, ple_norm, ple_out_norm, diff_sub_norm))
    lams = tuple(map(row, (diff_lambda_q1, diff_lambda_k1, diff_lambda_q2, diff_lambda_k2)))
    qk_gain = jnp.broadcast_to(jnp.stack([diff_q_norm, diff_k_norm])[..., None],
                               (2, DEPTH, HEAD_DIM, HEAD_DIM))
    cos_t, sin_t = _rope_tables(seq)
    p = _to_segment_order(p.reshape(DEPTH, t, PLE_DIM), 1)

    h = _to_segment_order(x.reshape(t, D_MODEL), 0)
    for layer in range(DEPTH):
        lambda_init = 0.8 - 0.6 * math.exp(-0.3 * layer)
        h = _ffn(h, ffn1_norm, ffn1_w_gu, ffn1_w_down, layer, segment_order_in=(layer == 0))

        qkv, u = _mixer_proj(h, mix_norm, w_in, qk_gain, cos_t, sin_t, layer, seq)
        y_a = _sb_attention(qkv, batch, seq)
        y_b = _diff_attention(lams, diff_sub_norm, qkv, layer, batch, seq, lambda_init)
        merged = _merge(u, w_in, y_a, y_b, w_branch_a, w_branch_b, layer)
        h = _out_proj(h, merged, w_out, layer)

        h = _ffn(h, ffn2_norm, ffn2_w_gu, ffn2_w_down, layer)
        h = _ple(h, ple_norm, ple_w_gate, p, ple_w_proj, ple_out_norm, layer,
                 segment_order_out=(layer == DEPTH - 1))
    return _from_segment_order(h, 0).reshape(batch, seq, D_MODEL)
```

```python
import functools
import math

import jax
import jax.numpy as jnp
from jax import lax
from jax.experimental import pallas as pl
from jax.experimental.pallas import tpu as pltpu

D_MODEL = 2048
DEPTH = 2
HEAD_DIM = 128
SB_HEADS = 8
DIFF_HEADS = 4
DIFF_V_DIM = 2 * HEAD_DIM
SB_WIDTH = SB_HEADS * HEAD_DIM
DIFF_QK_WIDTH = DIFF_HEADS * 2 * HEAD_DIM
DIFF_V_WIDTH = DIFF_HEADS * DIFF_V_DIM
D_FF = 5632
PLE_DIM = 256
ROPE_THETA = 500000.0
ROT_DIM = HEAD_DIM // 4
EPS = 1e-6
FFN_RES_WEIGHT = 0.5
QK_SCALE = HEAD_DIM ** -0.5 * math.log2(math.e)
SB_Q_SCALE = -0.5 * HEAD_DIM ** -0.5

OFF_DF_Q = 3 * SB_WIDTH
OFF_DF_V = OFF_DF_Q + 2 * DIFF_QK_WIDTH
OFF_G_A = OFF_DF_V + DIFF_V_WIDTH
OFF_G_B = OFF_G_A + D_MODEL

F32 = jnp.float32
BF16 = jnp.bfloat16

VMEM_LIMIT_BYTES = 56 * 1024 * 1024
SUBLANES = 8
MXU_DIM = 256
ATTN_BLOCK = 512
MASK_VALUE = -1e30

MIX_BLOCK = 1024
SB_Q_BLOCK, SB_K_BLOCK, SB_V_BLOCK, DF_Q_BLOCK, DF_K_BLOCK, DF_V_BLOCK = range(6)
HEADS_PER_MIX_BLOCK = MIX_BLOCK // HEAD_DIM
SB_HEADS_PER_STEP = 4
DIFF_HEADS_PER_STEP = 2


def _params(*sem):
    return pltpu.CompilerParams(dimension_semantics=sem, vmem_limit_bytes=VMEM_LIMIT_BYTES)


def _rms(x, g):
    ms = jnp.mean(x * x, axis=-1, keepdims=True)
    return x * lax.rsqrt(ms + EPS) * g


def _dot(a, b):
    return jnp.dot(a, b, preferred_element_type=F32)


def _dot_nt(a, b):
    return lax.dot_general(a, b, (((1,), (1,)), ((), ())), preferred_element_type=F32)


def _dot_tn_nt(a, b):
    return lax.dot_general(a, b, (((0,), (1,)), ((), ())), preferred_element_type=F32)


def _dot_tn(a, b):
    return lax.dot_general(a, b, (((0,), (0,)), ((), ())), preferred_element_type=F32)


def _flip_row_groups(x):
    n = x.shape[0] // SUBLANES
    return jnp.concatenate(
        [x[(n - 1 - g) * SUBLANES:(n - g) * SUBLANES] for g in range(n)], axis=0)


def _block_position(idx):
    seg = ATTN_BLOCK // SUBLANES
    return (idx % SUBLANES) * seg + (seg - 1 - idx // SUBLANES)


def _ffn_kernel(h_ref, g_ref, wg_ref, wu_ref, wd_ref, o_ref, u_ref, *, tn, segment_order_in):
    @pl.when(pl.program_id(1) == 0)
    def _():
        h = h_ref[...]
        if segment_order_in:
            h = jnp.concatenate([_flip_row_groups(h[b:b + ATTN_BLOCK])
                                 for b in range(0, h.shape[0], ATTN_BLOCK)], axis=0)
        u_ref[...] = _rms(h, g_ref[...]).astype(BF16)
        o_ref[...] = h

    u = u_ref[...]
    acts = []
    for c in range(0, wg_ref.shape[1], MXU_DIM):
        sub = slice(c, c + MXU_DIM)
        gate = _dot(u, wg_ref[:, sub])
        up = _dot(u, wu_ref[:, sub])
        acts.append((sub, (gate * jax.nn.sigmoid(gate) * up).astype(BF16)))
    for sub, a in acts:
        for n in range(D_MODEL // tn):
            cols = slice(n * tn, (n + 1) * tn)
            o_ref[:, cols] += FFN_RES_WEIGHT * _dot(a, wd_ref[sub, cols])


def _ffn(h, gain, w_gu, w_down, layer, *, segment_order_in=False, tm=1024, tf=512, tn=512):
    t = h.shape[0]
    assert tm % ATTN_BLOCK == 0
    nf = D_FF // tf
    return pl.pallas_call(
        functools.partial(_ffn_kernel, tn=tn, segment_order_in=segment_order_in),
        grid=(t // tm, nf),
        in_specs=[
            pl.BlockSpec((tm, D_MODEL), lambda i, k: (i, 0)),
            pl.BlockSpec((None, 1, D_MODEL), lambda i, k: (layer, 0, 0)),
            pl.BlockSpec((None, D_MODEL, tf), lambda i, k: (layer, 0, k)),
            pl.BlockSpec((None, D_MODEL, tf), lambda i, k: (layer, 0, k + nf)),
            pl.BlockSpec((None, tf, D_MODEL), lambda i, k: (layer, k, 0)),
        ],
        out_specs=pl.BlockSpec((tm, D_MODEL), lambda i, k: (i, 0)),
        out_shape=jax.ShapeDtypeStruct((t, D_MODEL), F32),
        scratch_shapes=[pltpu.VMEM((tm, D_MODEL), BF16)],
        compiler_params=_params("parallel", "arbitrary"),
        name="ffn",
    )(h, gain, w_gu, w_gu, w_down)


def _mixer_proj_kernel(h_ref, g_ref, w_ref, qkg_ref, cos_ref, sin_ref, o_ref, u_ref, *, tm):
    j = pl.program_id(1)

    @pl.when(j == 0)
    def _():
        u_ref[...] = _rms(h_ref[...], g_ref[...]).astype(BF16)

    u = u_ref[...]

    def store(head, y):
        for b in range(tm // ATTN_BLOCK):
            o_ref[head, b] = y[:, b * ATTN_BLOCK:(b + 1) * ATTN_BLOCK].astype(BF16)

    def run(finish):
        starts = list(range(0, MIX_BLOCK, MXU_DIM))
        r = _dot_tn_nt(w_ref[:, starts[0]:starts[0] + MXU_DIM], u)
        for prev, r0 in zip(starts, starts[1:]):
            r_next = _dot_tn_nt(w_ref[:, r0:r0 + MXU_DIM], u)
            finish(r, prev)
            r = r_next
        finish(r, starts[-1])

    def heads_of(r, r0):
        for c in range(MXU_DIM // HEAD_DIM):
            yield r0 // HEAD_DIM + c, r[c * HEAD_DIM:(c + 1) * HEAD_DIM, :]

    is_rotary = jnp.logical_or(j == DF_Q_BLOCK, j == DF_K_BLOCK)

    @pl.when(jnp.logical_not(is_rotary))
    def _():
        scale = jnp.where(j == SB_Q_BLOCK, SB_Q_SCALE, 1.0).astype(F32)

        def finish(r, r0):
            for head, x in heads_of(r, r0):
                store(head, x * scale)

        run(finish)

    @pl.when(is_rotary)
    def _():
        gain = jnp.concatenate([qkg_ref[...]] * (tm // HEAD_DIM), axis=1)
        gain = gain * jnp.where(j == DF_Q_BLOCK, QK_SCALE, 1.0).astype(F32)
        cos = cos_ref[...]
        sin = sin_ref[...]
        half = ROT_DIM // 2

        def finish(r, r0):
            for head, x in heads_of(r, r0):
                ms = jnp.mean(x * x, axis=0, keepdims=True)
                y = x * lax.rsqrt(ms + EPS) * gain
                y1, y2 = y[:half], y[half:ROT_DIM]
                store(head, jnp.concatenate(
                    [y1 * cos - y2 * sin, y2 * cos + y1 * sin, y[ROT_DIM:]], axis=0))

        run(finish)


def _mixer_proj(h, gain, w_in, qk_gain, cos_t, sin_t, layer, seq, *, tm=1024):
    t = h.shape[0]
    nblocks = OFF_G_A // MIX_BLOCK
    nseq = seq // tm
    return pl.pallas_call(
        functools.partial(_mixer_proj_kernel, tm=tm),
        grid=(t // tm, nblocks),
        in_specs=[
            pl.BlockSpec((tm, D_MODEL), lambda i, j: (i, 0)),
            pl.BlockSpec((None, 1, D_MODEL), lambda i, j: (layer, 0, 0)),
            pl.BlockSpec((None, D_MODEL, MIX_BLOCK), lambda i, j: (layer, 0, j)),
            pl.BlockSpec((None, None, HEAD_DIM, HEAD_DIM),
                         lambda i, j: (jnp.clip(j - DF_Q_BLOCK, 0, 1), layer, 0, 0)),
            pl.BlockSpec((ROT_DIM // 2, tm), lambda i, j: (0, i % nseq)),
            pl.BlockSpec((ROT_DIM // 2, tm), lambda i, j: (0, i % nseq)),
        ],
        out_specs=[pl.BlockSpec((HEADS_PER_MIX_BLOCK, tm // ATTN_BLOCK, HEAD_DIM, ATTN_BLOCK),
                                lambda i, j: (j, i, 0, 0)),
                   pl.BlockSpec((tm, D_MODEL), lambda i, j: (i, 0))],
        out_shape=[jax.ShapeDtypeStruct(
            (nblocks * HEADS_PER_MIX_BLOCK, t // ATTN_BLOCK, HEAD_DIM, ATTN_BLOCK), BF16),
            jax.ShapeDtypeStruct((t, D_MODEL), BF16)],
        compiler_params=_params("parallel", "arbitrary"),
        name="mixer_proj",
    )(h, gain, w_in, qk_gain, cos_t, sin_t)


def _sb_kernel(q_ref, k_ref, v_ref, o_ref, z_scr, w_scr, acc_scr):
    tq = tk = ATTN_BLOCK
    heads = range(SB_HEADS_PER_STEP)
    ngroups = tk // SUBLANES
    seg = tk // SUBLANES
    group = lambda g: slice(g * SUBLANES, (g + 1) * SUBLANES)

    def logits(qi, kb):
        for hd in heads:
            z_scr[hd] = _dot_tn(k_ref[hd, kb], q_ref[hd, qi])

    def weights(masked):
        if masked:
            sub = lax.broadcasted_iota(jnp.int32, (SUBLANES, tq), 0)
            t_pos = _block_position(lax.broadcasted_iota(jnp.int32, (SUBLANES, tq), 1))
        run = [jnp.ones((SUBLANES, tq), F32) for _ in heads]
        for g in range(ngroups):
            for hd in heads:
                th = 0.5 * jnp.tanh(z_scr[hd, group(g), :])
                if masked:
                    th = jnp.where(sub * seg + (seg - 1 - g) < t_pos, th, 0.5)
                w_scr[hd, group(g), :] = (0.5 - th) * run[hd]
                run[hd] = run[hd] * (0.5 + th)
        return tuple(run)

    def accumulate(kb, totals, carries):
        out = []
        for hd in heads:
            carry = carries[hd]
            offs = [None] * SUBLANES
            for i in reversed(range(SUBLANES)):
                offs[i] = carry
                carry = carry * totals[hd][i:i + 1, :]
            off = jnp.concatenate(offs, axis=0)
            attn = jnp.concatenate([w_scr[hd, group(g), :] * off for g in range(ngroups)],
                                   axis=0).astype(BF16)
            acc_scr[hd] += _dot(v_ref[hd, kb], attn)
            out.append(carry)
        return tuple(out)

    def query_block(qi, _):
        def body(it, state):
            totals, carries = state
            kb = qi - it
            carries = accumulate(kb, totals, carries)
            totals = weights(False)
            logits(qi, jnp.maximum(kb - 2, 0))
            return totals, carries

        acc_scr[...] = jnp.zeros_like(acc_scr)
        logits(qi, qi)
        totals = weights(True)
        logits(qi, jnp.maximum(qi - 1, 0))
        ones = tuple(jnp.ones((1, tq), F32) for _ in heads)
        totals, carries = lax.fori_loop(0, qi, body, (totals, ones))
        accumulate(0, totals, carries)
        for hd in heads:
            o_ref[pl.ds(pl.multiple_of(qi * tq, tq), tq), hd * HEAD_DIM:(hd + 1) * HEAD_DIM] = (
                acc_scr[hd].T.astype(BF16))
        return 0

    lax.fori_loop(0, q_ref.shape[1], query_block, 0)


def _sb_attention(qkv, batch, seq):
    nq = seq // ATTN_BLOCK
    t = batch * seq
    hps = SB_HEADS_PER_STEP
    head_seq = (hps, nq, HEAD_DIM, ATTN_BLOCK)
    first = lambda block: block * HEADS_PER_MIX_BLOCK // hps
    return pl.pallas_call(
        _sb_kernel,
        grid=(batch, SB_HEADS // hps),
        in_specs=[
            pl.BlockSpec(head_seq, lambda b, h: (first(SB_Q_BLOCK) + h, b, 0, 0)),
            pl.BlockSpec(head_seq, lambda b, h: (first(SB_K_BLOCK) + h, b, 0, 0)),
            pl.BlockSpec(head_seq, lambda b, h: (first(SB_V_BLOCK) + h, b, 0, 0)),
        ],
        out_specs=pl.BlockSpec((seq, hps * HEAD_DIM), lambda b, h: (b, h)),
        out_shape=jax.ShapeDtypeStruct((t, SB_WIDTH), BF16),
        scratch_shapes=[pltpu.VMEM((hps, ATTN_BLOCK, ATTN_BLOCK), F32),
                        pltpu.VMEM((hps, ATTN_BLOCK, ATTN_BLOCK), F32),
                        pltpu.VMEM((hps, HEAD_DIM, ATTN_BLOCK), F32)],
        compiler_params=_params("parallel", "parallel"),
        name="sb_attention",
    )(qkv, qkv, qkv)


def _diff_kernel(lq1_ref, lk1_ref, lq2_ref, lk2_ref, sub_ref, q_ref, k_ref, v_ref, o_ref,
                 s_scr, acc_scr, *, lambda_init):
    tq = tk = ATTN_BLOCK
    chains = [(hd, half) for hd in range(DIFF_HEADS_PER_STEP) for half in range(2)]
    entry = lambda hd, half: 2 * hd + half

    def logits(qi, kb, masked):
        maxima = []
        for hd, half in chains:
            c = entry(hd, half)
            s = _dot_tn(k_ref[c, kb], q_ref[c, qi])
            if masked:
                s_pos = _block_position(lax.broadcasted_iota(jnp.int32, (tk, tq), 0))
                t_pos = _block_position(lax.broadcasted_iota(jnp.int32, (tk, tq), 1))
                s = jnp.where(s_pos <= t_pos, s, MASK_VALUE)
            s_scr[c] = s
            maxima.append(jnp.max(s, axis=0, keepdims=True))
        return tuple(maxima)

    def accumulate(kb, maxima, stats):
        new_stats = []
        for hd, half in chains:
            c = entry(hd, half)
            v_t = jnp.concatenate([v_ref[entry(hd, 0), kb], v_ref[entry(hd, 1), kb]], axis=0)
            m, l = stats[c]
            m_new = jnp.maximum(m, maxima[c])
            alpha = jnp.exp2(m - m_new)
            p = jnp.exp2(s_scr[c] - m_new)
            l = alpha * l + jnp.sum(p, axis=0, keepdims=True)
            acc_scr[c] = alpha * acc_scr[c] + _dot(v_t, p.astype(BF16))
            new_stats.append((m_new, l))
        return tuple(new_stats)

    lam = (jnp.exp(jnp.sum(lq1_ref[...] * lk1_ref[...], axis=1, keepdims=True))
           - jnp.exp(jnp.sum(lq2_ref[...] * lk2_ref[...], axis=1, keepdims=True))
           + lambda_init)
    nq = q_ref.shape[1]

    def query_block(qi, maxima):
        def body(it, state):
            maxima, stats = state
            kb = qi - it
            stats = accumulate(kb, maxima, stats)
            return logits(qi, kb - 1, False), stats

        init = (jnp.full((1, tq), MASK_VALUE, F32), jnp.zeros((1, tq), F32))
        maxima, stats = lax.fori_loop(0, qi, body, (maxima, (init,) * len(chains)))
        stats = accumulate(0, maxima, stats)
        for hd in range(DIFF_HEADS_PER_STEP):
            (_, l1), (_, l2) = stats[entry(hd, 0)], stats[entry(hd, 1)]
            y = (acc_scr[entry(hd, 0)] / l1 - lam * (acc_scr[entry(hd, 1)] / l2)).T
            o_ref[pl.ds(pl.multiple_of(qi * tq, tq), tq), hd * DIFF_V_DIM:(hd + 1) * DIFF_V_DIM] = (
                _rms(y, sub_ref[...]) * (1.0 - lambda_init)).astype(BF16)
        acc_scr[...] = jnp.zeros_like(acc_scr)
        nxt = jnp.minimum(qi + 1, nq - 1)
        return logits(nxt, nxt, True)

    acc_scr[...] = jnp.zeros_like(acc_scr)
    lax.fori_loop(0, nq, query_block, logits(0, 0, True))


def _diff_attention(lams, sub_gain, qkv, layer, batch, seq, lambda_init):
    nq = seq // ATTN_BLOCK
    t = batch * seq
    hps = DIFF_HEADS_PER_STEP
    head_seq = (2 * hps, nq, HEAD_DIM, ATTN_BLOCK)
    first = lambda block: block * HEADS_PER_MIX_BLOCK // (2 * hps)
    lam_spec = pl.BlockSpec((None, 1, HEAD_DIM), lambda b, h: (layer, 0, 0))
    return pl.pallas_call(
        functools.partial(_diff_kernel, lambda_init=lambda_init),
        grid=(batch, DIFF_HEADS // hps),
        in_specs=[
            lam_spec, lam_spec, lam_spec, lam_spec,
            pl.BlockSpec((None, 1, DIFF_V_DIM), lambda b, h: (layer, 0, 0)),
            pl.BlockSpec(head_seq, lambda b, h: (first(DF_Q_BLOCK) + h, b, 0, 0)),
            pl.BlockSpec(head_seq, lambda b, h: (first(DF_K_BLOCK) + h, b, 0, 0)),
            pl.BlockSpec(head_seq, lambda b, h: (first(DF_V_BLOCK) + h, b, 0, 0)),
        ],
        out_specs=pl.BlockSpec((seq, hps * DIFF_V_DIM), lambda b, h: (b, h)),
        out_shape=jax.ShapeDtypeStruct((t, DIFF_V_WIDTH), BF16),
        scratch_shapes=[pltpu.VMEM((2 * hps, ATTN_BLOCK, ATTN_BLOCK), F32),
                        pltpu.VMEM((2 * hps, DIFF_V_DIM, ATTN_BLOCK), F32)],
        compiler_params=_params("parallel", "parallel"),
        name="diff_attention",
    )(*lams, sub_gain, qkv, qkv, qkv)


def _merge_kernel(u_ref, wga_ref, wgb_ref, ya_ref, yb_ref, wa_ref, wb_ref, o_ref):
    u = u_ref[...]
    merged = (jax.nn.sigmoid(_dot(u, wga_ref[...])) * _dot(ya_ref[...], wa_ref[...])
              + jax.nn.sigmoid(_dot(u, wgb_ref[...])) * _dot(yb_ref[...], wb_ref[...]))
    o_ref[...] = merged.astype(BF16)


def _merge(u, w_in, ya, yb, wa, wb, layer, *, tm=1024, tn=512):
    t = u.shape[0]
    ga0 = OFF_G_A // tn
    gb0 = OFF_G_B // tn
    return pl.pallas_call(
        _merge_kernel,
        grid=(t // tm, D_MODEL // tn),
        in_specs=[
            pl.BlockSpec((tm, D_MODEL), lambda i, j: (i, 0)),
            pl.BlockSpec((None, D_MODEL, tn), lambda i, j: (layer, 0, ga0 + j)),
            pl.BlockSpec((None, D_MODEL, tn), lambda i, j: (layer, 0, gb0 + j)),
            pl.BlockSpec((tm, SB_WIDTH), lambda i, j: (i, 0)),
            pl.BlockSpec((tm, DIFF_V_WIDTH), lambda i, j: (i, 0)),
            pl.BlockSpec((None, SB_WIDTH, tn), lambda i, j: (layer, 0, j)),
            pl.BlockSpec((None, DIFF_V_WIDTH, tn), lambda i, j: (layer, 0, j)),
        ],
        out_specs=pl.BlockSpec((tm, tn), lambda i, j: (i, j)),
        out_shape=jax.ShapeDtypeStruct((t, D_MODEL), BF16),
        compiler_params=_params("parallel", "arbitrary"),
        name="merge",
    )(u, w_in, w_in, ya, yb, wa, wb)


def _out_kernel(h_ref, m_ref, w_ref, o_ref):
    o_ref[...] = h_ref[...] + _dot(m_ref[...], w_ref[...])


def _out_proj(h, merged, w_out, layer, *, tm=1024, tn=1024):
    t = h.shape[0]
    return pl.pallas_call(
        _out_kernel,
        grid=(t // tm, D_MODEL // tn),
        in_specs=[
            pl.BlockSpec((tm, tn), lambda i, j: (i, j)),
            pl.BlockSpec((tm, D_MODEL), lambda i, j: (i, 0)),
            pl.BlockSpec((None, D_MODEL, tn), lambda i, j: (layer, 0, j)),
        ],
        out_specs=pl.BlockSpec((tm, tn), lambda i, j: (i, j)),
        out_shape=jax.ShapeDtypeStruct((t, D_MODEL), F32),
        compiler_params=_params("parallel", "arbitrary"),
        name="out_proj",
    )(h, merged, w_out)


def _ple_kernel(h_ref, g_ref, wg_ref, p_ref, wp_ref, og_ref, o_ref, *, segment_order_out):
    h = h_ref[...]
    u = _rms(h, g_ref[...]).astype(BF16)
    gate = jax.nn.sigmoid(_dot(u, wg_ref[...]))
    p = _flip_row_groups(p_ref[...]).astype(BF16)
    out = h + gate * _rms(_dot(p, wp_ref[...]), og_ref[...])
    o_ref[...] = _flip_row_groups(out) if segment_order_out else out


def _ple(h, gain, w_gate, p, w_proj, out_gain, layer, *, segment_order_out=False):
    t = h.shape[0]
    tm = ATTN_BLOCK
    nt = t // tm
    return pl.pallas_call(
        functools.partial(_ple_kernel, segment_order_out=segment_order_out),
        grid=(nt,),
        in_specs=[
            pl.BlockSpec((tm, D_MODEL), lambda i: (i, 0)),
            pl.BlockSpec((None, 1, D_MODEL), lambda i: (layer, 0, 0)),
            pl.BlockSpec((None, D_MODEL, D_MODEL), lambda i: (layer, 0, 0)),
            pl.BlockSpec((None, tm, PLE_DIM), lambda i: (layer, i, 0)),
            pl.BlockSpec((None, PLE_DIM, D_MODEL), lambda i: (layer, 0, 0)),
            pl.BlockSpec((None, 1, D_MODEL), lambda i: (layer, 0, 0)),
        ],
        out_specs=pl.BlockSpec((tm, D_MODEL), lambda i: (i, 0)),
        out_shape=jax.ShapeDtypeStruct((t, D_MODEL), F32),
        compiler_params=_params("parallel"),
        name="ple",
    )(h, gain, w_gate, p, w_proj, out_gain)


def _to_segment_order(a, axis, reverse_positions=False):
    seg = ATTN_BLOCK // SUBLANES
    shape = a.shape
    a = a.reshape(shape[:axis] + (shape[axis] // ATTN_BLOCK, SUBLANES, seg) + shape[axis + 1:])
    if reverse_positions:
        a = jnp.flip(a, axis + 2)
    return jnp.swapaxes(a, axis + 1, axis + 2).reshape(shape)


def _from_segment_order(a, axis):
    seg = ATTN_BLOCK // SUBLANES
    shape = a.shape
    a = a.reshape(shape[:axis] + (shape[axis] // ATTN_BLOCK, seg, SUBLANES) + shape[axis + 1:])
    return jnp.swapaxes(a, axis + 1, axis + 2).reshape(shape)


def _rope_tables(seq):
    pos = jnp.arange(seq, dtype=F32)
    inv_freq = ROPE_THETA ** (-jnp.arange(0, ROT_DIM, 2, dtype=F32) / ROT_DIM)
    ang = _to_segment_order(pos[:, None] * inv_freq[None, :], 0, reverse_positions=True)
    return jnp.cos(ang).T, jnp.sin(ang).T


def kernel(x, p, ffn1_norm, ffn1_w_gu, ffn1_w_down, mix_norm, w_in, diff_q_norm, diff_k_norm,
           diff_lambda_q1, diff_lambda_k1, diff_lambda_q2, diff_lambda_k2, diff_sub_norm,
           w_branch_a, w_branch_b, w_out, ffn2_norm, ffn2_w_gu, ffn2_w_down, ple_norm,
           ple_w_gate, ple_w_proj, ple_out_norm):
    batch, seq, _ = x.shape
    t = batch * seq
    row = lambda a: a.reshape(DEPTH, 1, a.shape[-1])
    bf = lambda a: a.astype(BF16)

    ffn1_w_gu, ffn1_w_down, ffn2_w_gu, ffn2_w_down = map(
        bf, (ffn1_w_gu, ffn1_w_down, ffn2_w_gu, ffn2_w_down))
    w_in, w_branch_a, w_branch_b, w_out = map(bf, (w_in, w_branch_a, w_branch_b, w_out))
    ple_w_gate, ple_w_proj = bf(ple_w_gate), bf(ple_w_proj)
    ffn1_norm, mix_norm, ffn2_norm, ple_norm, ple_out_norm, diff_sub_norm = map(
        row, (ffn1_norm, mix_norm, ffn2_norm, ple_norm, ple_out_norm, diff_sub_norm))
    lams = tuple(map(row, (diff_lambda_q1, diff_lambda_k1, diff_lambda_q2, diff_lambda_k2)))
    qk_gain = jnp.broadcast_to(jnp.stack([diff_q_norm, diff_k_norm])[..., None],
                               (2, DEPTH, HEAD_DIM, HEAD_DIM))
    cos_t, sin_t = _rope_tables(seq)
    p = _to_segment_order(p.reshape(DEPTH, t, PLE_DIM), 1)

    h = _to_segment_order(x.reshape(t, D_MODEL), 0)
    for layer in range(DEPTH):
        lambda_init = 0.8 - 0.6 * math.exp(-0.3 * layer)
        h = _ffn(h, ffn1_norm, ffn1_w_gu, ffn1_w_down, layer, segment_order_in=(layer == 0))

        qkv, u = _mixer_proj(h, mix_norm, w_in, qk_gain, cos_t, sin_t, layer, seq)
        y_a = _sb_attention(qkv, batch, seq)
        y_b = _diff_attention(lams, diff_sub_norm, qkv, layer, batch, seq, lambda_init)
        merged = _merge(u, w_in, y_a, y_b, w_branch_a, w_branch_b, layer)
        h = _out_proj(h, merged, w_out, layer)

        h = _ffn(h, ffn2_norm, ffn2_w_gu, ffn2_w_down, layer)
        h = _ple(h, ple_norm, ple_w_gate, p, ple_w_proj, ple_out_norm, layer,
                 segment_order_out=(layer == DEPTH - 1))
    return _from_segment_order(h, 0).reshape(batch, seq, D_MODEL)
```

```python
import functools
import math

import jax
import jax.numpy as jnp
from jax import lax
from jax.experimental import pallas as pl
from jax.experimental.pallas import tpu as pltpu

D_MODEL = 2048
DEPTH = 2
HEAD_DIM = 128
SB_HEADS = 8
DIFF_HEADS = 4
DIFF_V_DIM = 2 * HEAD_DIM
SB_WIDTH = SB_HEADS * HEAD_DIM
DIFF_QK_WIDTH = DIFF_HEADS * 2 * HEAD_DIM
DIFF_V_WIDTH = DIFF_HEADS * DIFF_V_DIM
D_FF = 5632
PLE_DIM = 256
ROPE_THETA = 500000.0
ROT_DIM = HEAD_DIM // 4
EPS = 1e-6
FFN_RES_WEIGHT = 0.5
QK_SCALE = HEAD_DIM ** -0.5 * math.log2(math.e)
SB_Q_SCALE = -0.5 * HEAD_DIM ** -0.5

OFF_DF_Q = 3 * SB_WIDTH
OFF_DF_V = OFF_DF_Q + 2 * DIFF_QK_WIDTH
OFF_G_A = OFF_DF_V + DIFF_V_WIDTH
OFF_G_B = OFF_G_A + D_MODEL

F32 = jnp.float32
BF16 = jnp.bfloat16

VMEM_LIMIT_BYTES = 56 * 1024 * 1024
SUBLANES = 8
MXU_DIM = 256
ATTN_BLOCK = 512
MASK_VALUE = -1e30

MIX_BLOCK = 1024
SB_Q_BLOCK, SB_K_BLOCK, SB_V_BLOCK, DF_Q_BLOCK, DF_K_BLOCK, DF_V_BLOCK = range(6)
HEADS_PER_MIX_BLOCK = MIX_BLOCK // HEAD_DIM
SB_HEADS_PER_STEP = 4
DIFF_HEADS_PER_STEP = 2


def _params(*sem):
    return pltpu.CompilerParams(dimension_semantics=sem, vmem_limit_bytes=VMEM_LIMIT_BYTES)


def _rms(x, g):
    ms = jnp.mean(x * x, axis=-1, keepdims=True)
    return x * lax.rsqrt(ms + EPS) * g


def _dot(a, b):
    return jnp.dot(a, b, preferred_element_type=F32)


def _dot_nt(a, b):
    return lax.dot_general(a, b, (((1,), (1,)), ((), ())), preferred_element_type=F32)


def _dot_tn_nt(a, b):
    return lax.dot_general(a, b, (((0,), (1,)), ((), ())), preferred_element_type=F32)


def _dot_tn(a, b):
    return lax.dot_general(a, b, (((0,), (0,)), ((), ())), preferred_element_type=F32)


def _flip_row_groups(x):
    n = x.shape[0] // SUBLANES
    return jnp.concatenate(
        [x[(n - 1 - g) * SUBLANES:(n - g) * SUBLANES] for g in range(n)], axis=0)


def _block_position(idx):
    seg = ATTN_BLOCK // SUBLANES
    return (idx % SUBLANES) * seg + (seg - 1 - idx // SUBLANES)


def _ffn_kernel(*refs, tn, segment_order_in, cast_next):
    if cast_next:
        (h_ref, g_ref, wg_ref, wu_ref, wd_ref, next_gu_ref, next_down_ref,
         o_ref, next_gu_bf_ref, next_down_bf_ref, u_ref) = refs
        next_gu_bf_ref[...] = next_gu_ref[...].astype(BF16)
        next_down_bf_ref[...] = next_down_ref[...].astype(BF16)
    else:
        h_ref, g_ref, wg_ref, wu_ref, wd_ref, o_ref, u_ref = refs

    @pl.when(pl.program_id(1) == 0)
    def _():
        h = h_ref[...]
        if segment_order_in:
            h = jnp.concatenate([_flip_row_groups(h[b:b + ATTN_BLOCK])
                                 for b in range(0, h.shape[0], ATTN_BLOCK)], axis=0)
        u_ref[...] = _rms(h, g_ref[...]).astype(BF16)
        o_ref[...] = h

    u = u_ref[...]
    acts = []
    for c in range(0, wg_ref.shape[1], MXU_DIM):
        sub = slice(c, c + MXU_DIM)
        gate = _dot(u, wg_ref[:, sub])
        up = _dot(u, wu_ref[:, sub])
        acts.append((sub, (gate * jax.nn.sigmoid(gate) * up).astype(BF16)))
    for sub, a in acts:
        for n in range(D_MODEL // tn):
            cols = slice(n * tn, (n + 1) * tn)
            o_ref[:, cols] += FFN_RES_WEIGHT * _dot(a, wd_ref[sub, cols])


def _ffn(h, gain, layer, w_gu, w_down, next_weights=None, *, segment_order_in=False,
         tm=1024, tf=512, tn=512):
    t = h.shape[0]
    assert tm % ATTN_BLOCK == 0
    nt, nf = t // tm, D_FF // tf
    in_specs = [
        pl.BlockSpec((tm, D_MODEL), lambda i, k: (i, 0)),
        pl.BlockSpec((None, 1, D_MODEL), lambda i, k: (layer, 0, 0)),
        pl.BlockSpec((D_MODEL, tf), lambda i, k: (0, k)),
        pl.BlockSpec((D_MODEL, tf), lambda i, k: (0, k + nf)),
        pl.BlockSpec((tf, D_MODEL), lambda i, k: (k, 0)),
    ]
    out_specs = [pl.BlockSpec((tm, D_MODEL), lambda i, k: (i, 0))]
    out_shape = [jax.ShapeDtypeStruct((t, D_MODEL), F32)]
    args = [h, gain, w_gu, w_gu, w_down]
    if next_weights is not None:
        next_gu, next_down, next_layer = next_weights
        gu_slab, down_slab = (D_MODEL // nt, 2 * D_FF // nf), (D_FF // nf, D_MODEL // nt)
        in_specs += [pl.BlockSpec((None,) + gu_slab, lambda i, k: (next_layer, i, k)),
                     pl.BlockSpec((None,) + down_slab, lambda i, k: (next_layer, k, i))]
        out_specs += [pl.BlockSpec(gu_slab, lambda i, k: (i, k)),
                      pl.BlockSpec(down_slab, lambda i, k: (k, i))]
        out_shape += [jax.ShapeDtypeStruct((D_MODEL, 2 * D_FF), BF16),
                      jax.ShapeDtypeStruct((D_FF, D_MODEL), BF16)]
        args += [next_gu, next_down]
    out = pl.pallas_call(
        functools.partial(_ffn_kernel, tn=tn, segment_order_in=segment_order_in,
                          cast_next=next_weights is not None),
        grid=(nt, nf),
        in_specs=in_specs,
        out_specs=out_specs,
        out_shape=out_shape,
        scratch_shapes=[pltpu.VMEM((tm, D_MODEL), BF16)],
        compiler_params=_params("parallel", "arbitrary"),
        name="ffn",
    )(*args)
    return out[0], tuple(out[1:])


def _mixer_proj_kernel(h_ref, g_ref, w_ref, qkg_ref, cos_ref, sin_ref, o_ref, u_ref, *, tm):
    j = pl.program_id(1)

    @pl.when(j == 0)
    def _():
        u_ref[...] = _rms(h_ref[...], g_ref[...]).astype(BF16)

    u = u_ref[...]

    def store(head, y):
        for b in range(tm // ATTN_BLOCK):
            o_ref[head, b] = y[:, b * ATTN_BLOCK:(b + 1) * ATTN_BLOCK].astype(BF16)

    def run(finish):
        starts = list(range(0, MIX_BLOCK, MXU_DIM))
        r = _dot_tn_nt(w_ref[:, starts[0]:starts[0] + MXU_DIM], u)
        for prev, r0 in zip(starts, starts[1:]):
            r_next = _dot_tn_nt(w_ref[:, r0:r0 + MXU_DIM], u)
            finish(r, prev)
            r = r_next
        finish(r, starts[-1])

    def heads_of(r, r0):
        for c in range(MXU_DIM // HEAD_DIM):
            yield r0 // HEAD_DIM + c, r[c * HEAD_DIM:(c + 1) * HEAD_DIM, :]

    is_rotary = jnp.logical_or(j == DF_Q_BLOCK, j == DF_K_BLOCK)

    @pl.when(jnp.logical_not(is_rotary))
    def _():
        scale = jnp.where(j == SB_Q_BLOCK, SB_Q_SCALE, 1.0).astype(F32)

        def finish(r, r0):
            for head, x in heads_of(r, r0):
                store(head, x * scale)

        run(finish)

    @pl.when(is_rotary)
    def _():
        gain = jnp.concatenate([qkg_ref[...]] * (tm // HEAD_DIM), axis=1)
        gain = gain * jnp.where(j == DF_Q_BLOCK, QK_SCALE, 1.0).astype(F32)
        cos = cos_ref[...]
        sin = sin_ref[...]
        half = ROT_DIM // 2

        def finish(r, r0):
            for head, x in heads_of(r, r0):
                ms = jnp.mean(x * x, axis=0, keepdims=True)
                y = x * lax.rsqrt(ms + EPS) * gain
                y1, y2 = y[:half], y[half:ROT_DIM]
                store(head, jnp.concatenate(
                    [y1 * cos - y2 * sin, y2 * cos + y1 * sin, y[ROT_DIM:]], axis=0))

        run(finish)


def _mixer_proj(h, gain, w_in, qk_gain, cos_t, sin_t, layer, seq, *, tm=1024):
    t = h.shape[0]
    nblocks = OFF_G_A // MIX_BLOCK
    nseq = seq // tm
    return pl.pallas_call(
        functools.partial(_mixer_proj_kernel, tm=tm),
        grid=(t // tm, nblocks),
        in_specs=[
            pl.BlockSpec((tm, D_MODEL), lambda i, j: (i, 0)),
            pl.BlockSpec((None, 1, D_MODEL), lambda i, j: (layer, 0, 0)),
            pl.BlockSpec((None, D_MODEL, MIX_BLOCK), lambda i, j: (layer, 0, j)),
            pl.BlockSpec((None, None, HEAD_DIM, HEAD_DIM),
                         lambda i, j: (jnp.clip(j - DF_Q_BLOCK, 0, 1), layer, 0, 0)),
            pl.BlockSpec((ROT_DIM // 2, tm), lambda i, j: (0, i % nseq)),
            pl.BlockSpec((ROT_DIM // 2, tm), lambda i, j: (0, i % nseq)),
        ],
        out_specs=[pl.BlockSpec((HEADS_PER_MIX_BLOCK, tm // ATTN_BLOCK, HEAD_DIM, ATTN_BLOCK),
                                lambda i, j: (j, i, 0, 0)),
                   pl.BlockSpec((tm, D_MODEL), lambda i, j: (i, 0))],
        out_shape=[jax.ShapeDtypeStruct(
            (nblocks * HEADS_PER_MIX_BLOCK, t // ATTN_BLOCK, HEAD_DIM, ATTN_BLOCK), BF16),
            jax.ShapeDtypeStruct((t, D_MODEL), BF16)],
        compiler_params=_params("parallel", "arbitrary"),
        name="mixer_proj",
    )(h, gain, w_in, qk_gain, cos_t, sin_t)


def _sb_kernel(q_ref, k_ref, v_ref, o_ref, z_scr, w_scr, acc_scr):
    tq = tk = ATTN_BLOCK
    heads = range(SB_HEADS_PER_STEP)
    ngroups = tk // SUBLANES
    seg = tk // SUBLANES
    group = lambda g: slice(g * SUBLANES, (g + 1) * SUBLANES)

    def logits(qi, kb):
        for hd in heads:
            z_scr[hd] = _dot_tn(k_ref[hd, kb], q_ref[hd, qi])

    def weights(masked):
        if masked:
            sub = lax.broadcasted_iota(jnp.int32, (SUBLANES, tq), 0)
            t_pos = _block_position(lax.broadcasted_iota(jnp.int32, (SUBLANES, tq), 1))
        run = [jnp.ones((SUBLANES, tq), F32) for _ in heads]
        for g in range(ngroups):
            for hd in heads:
                th = 0.5 * jnp.tanh(z_scr[hd, group(g), :])
                if masked:
                    th = jnp.where(sub * seg + (seg - 1 - g) < t_pos, th, 0.5)
                w_scr[hd, group(g), :] = (0.5 - th) * run[hd]
                run[hd] = run[hd] * (0.5 + th)
        return tuple(run)

    def accumulate(kb, totals, carries):
        out = []
        for hd in heads:
            carry = carries[hd]
            offs = [None] * SUBLANES
            for i in reversed(range(SUBLANES)):
                offs[i] = carry
                carry = carry * totals[hd][i:i + 1, :]
            off = jnp.concatenate(offs, axis=0)
            attn = jnp.concatenate([w_scr[hd, group(g), :] * off for g in range(ngroups)],
                                   axis=0).astype(BF16)
            acc_scr[hd] += _dot(v_ref[hd, kb], attn)
            out.append(carry)
        return tuple(out)

    def query_block(qi, _):
        def body(it, state):
            totals, carries = state
            kb = qi - it
            carries = accumulate(kb, totals, carries)
            totals = weights(False)
            logits(qi, jnp.maximum(kb - 2, 0))
            return totals, carries

        acc_scr[...] = jnp.zeros_like(acc_scr)
        logits(qi, qi)
        totals = weights(True)
        logits(qi, jnp.maximum(qi - 1, 0))
        ones = tuple(jnp.ones((1, tq), F32) for _ in heads)
        totals, carries = lax.fori_loop(0, qi, body, (totals, ones))
        accumulate(0, totals, carries)
        for hd in heads:
            o_ref[pl.ds(pl.multiple_of(qi * tq, tq), tq), hd * HEAD_DIM:(hd + 1) * HEAD_DIM] = (
                acc_scr[hd].T.astype(BF16))
        return 0

    lax.fori_loop(0, q_ref.shape[1], query_block, 0)


def _sb_attention(qkv, batch, seq):
    nq = seq // ATTN_BLOCK
    t = batch * seq
    hps = SB_HEADS_PER_STEP
    head_seq = (hps, nq, HEAD_DIM, ATTN_BLOCK)
    first = lambda block: block * HEADS_PER_MIX_BLOCK // hps
    return pl.pallas_call(
        _sb_kernel,
        grid=(batch, SB_HEADS // hps),
        in_specs=[
            pl.BlockSpec(head_seq, lambda b, h: (first(SB_Q_BLOCK) + h, b, 0, 0)),
            pl.BlockSpec(head_seq, lambda b, h: (first(SB_K_BLOCK) + h, b, 0, 0)),
            pl.BlockSpec(head_seq, lambda b, h: (first(SB_V_BLOCK) + h, b, 0, 0)),
        ],
        out_specs=pl.BlockSpec((seq, hps * HEAD_DIM), lambda b, h: (b, h)),
        out_shape=jax.ShapeDtypeStruct((t, SB_WIDTH), BF16),
        scratch_shapes=[pltpu.VMEM((hps, ATTN_BLOCK, ATTN_BLOCK), F32),
                        pltpu.VMEM((hps, ATTN_BLOCK, ATTN_BLOCK), F32),
                        pltpu.VMEM((hps, HEAD_DIM, ATTN_BLOCK), F32)],
        compiler_params=_params("parallel", "parallel"),
        name="sb_attention",
    )(qkv, qkv, qkv)


def _diff_kernel(lq1_ref, lk1_ref, lq2_ref, lk2_ref, sub_ref, q_ref, k_ref, v_ref, o_ref,
                 s_scr, acc_scr, *, lambda_init):
    tq = tk = ATTN_BLOCK
    chains = [(hd, half) for hd in range(DIFF_HEADS_PER_STEP) for half in range(2)]
    entry = lambda hd, half: 2 * hd + half

    def logits(qi, kb, masked):
        maxima = []
        for hd, half in chains:
            c = entry(hd, half)
            s = _dot_tn(k_ref[c, kb], q_ref[c, qi])
            if masked:
                s_pos = _block_position(lax.broadcasted_iota(jnp.int32, (tk, tq), 0))
                t_pos = _block_position(lax.broadcasted_iota(jnp.int32, (tk, tq), 1))
                s = jnp.where(s_pos <= t_pos, s, MASK_VALUE)
            s_scr[c] = s
            maxima.append(jnp.max(s, axis=0, keepdims=True))
        return tuple(maxima)

    def accumulate(kb, maxima, stats):
        new_stats = []
        for hd, half in chains:
            c = entry(hd, half)
            v_t = jnp.concatenate([v_ref[entry(hd, 0), kb], v_ref[entry(hd, 1), kb]], axis=0)
            m, l = stats[c]
            m_new = jnp.maximum(m, maxima[c])
            alpha = jnp.exp2(m - m_new)
            p = jnp.exp2(s_scr[c] - m_new)
            l = alpha * l + jnp.sum(p, axis=0, keepdims=True)
            acc_scr[c] = alpha * acc_scr[c] + _dot(v_t, p.astype(BF16))
            new_stats.append((m_new, l))
        return tuple(new_stats)

    lam = (jnp.exp(jnp.sum(lq1_ref[...] * lk1_ref[...], axis=1, keepdims=True))
           - jnp.exp(jnp.sum(lq2_ref[...] * lk2_ref[...], axis=1, keepdims=True))
           + lambda_init)
    nq = q_ref.shape[1]

    def query_block(qi, maxima):
        def body(it, state):
            maxima, stats = state
            kb = qi - it
            stats = accumulate(kb, maxima, stats)
            return logits(qi, kb - 1, False), stats

        init = (jnp.full((1, tq), MASK_VALUE, F32), jnp.zeros((1, tq), F32))
        maxima, stats = lax.fori_loop(0, qi, body, (maxima, (init,) * len(chains)))
        stats = accumulate(0, maxima, stats)
        for hd in range(DIFF_HEADS_PER_STEP):
            (_, l1), (_, l2) = stats[entry(hd, 0)], stats[entry(hd, 1)]
            y = (acc_scr[entry(hd, 0)] / l1 - lam * (acc_scr[entry(hd, 1)] / l2)).T
            o_ref[pl.ds(pl.multiple_of(qi * tq, tq), tq), hd * DIFF_V_DIM:(hd + 1) * DIFF_V_DIM] = (
                _rms(y, sub_ref[...]) * (1.0 - lambda_init)).astype(BF16)
        acc_scr[...] = jnp.zeros_like(acc_scr)
        nxt = jnp.minimum(qi + 1, nq - 1)
        return logits(nxt, nxt, True)

    acc_scr[...] = jnp.zeros_like(acc_scr)
    lax.fori_loop(0, nq, query_block, logits(0, 0, True))


def _diff_attention(lams, sub_gain, qkv, layer, batch, seq, lambda_init):
    nq = seq // ATTN_BLOCK
    t = batch * seq
    hps = DIFF_HEADS_PER_STEP
    head_seq = (2 * hps, nq, HEAD_DIM, ATTN_BLOCK)
    first = lambda block: block * HEADS_PER_MIX_BLOCK // (2 * hps)
    lam_spec = pl.BlockSpec((None, 1, HEAD_DIM), lambda b, h: (layer, 0, 0))
    return pl.pallas_call(
        functools.partial(_diff_kernel, lambda_init=lambda_init),
        grid=(batch, DIFF_HEADS // hps),
        in_specs=[
            lam_spec, lam_spec, lam_spec, lam_spec,
            pl.BlockSpec((None, 1, DIFF_V_DIM), lambda b, h: (layer, 0, 0)),
            pl.BlockSpec(head_seq, lambda b, h: (first(DF_Q_BLOCK) + h, b, 0, 0)),
            pl.BlockSpec(head_seq, lambda b, h: (first(DF_K_BLOCK) + h, b, 0, 0)),
            pl.BlockSpec(head_seq, lambda b, h: (first(DF_V_BLOCK) + h, b, 0, 0)),
        ],
        out_specs=pl.BlockSpec((seq, hps * DIFF_V_DIM), lambda b, h: (b, h)),
        out_shape=jax.ShapeDtypeStruct((t, DIFF_V_WIDTH), BF16),
        scratch_shapes=[pltpu.VMEM((2 * hps, ATTN_BLOCK, ATTN_BLOCK), F32),
                        pltpu.VMEM((2 * hps, DIFF_V_DIM, ATTN_BLOCK), F32)],
        compiler_params=_params("parallel", "parallel"),
        name="diff_attention",
    )(*lams, sub_gain, qkv, qkv, qkv)


def _merge_kernel(u_ref, wga_ref, wgb_ref, ya_ref, yb_ref, wa_ref, wb_ref, o_ref):
    u = u_ref[...]
    merged = (jax.nn.sigmoid(_dot(u, wga_ref[...])) * _dot(ya_ref[...], wa_ref[...])
              + jax.nn.sigmoid(_dot(u, wgb_ref[...])) * _dot(yb_ref[...], wb_ref[...]))
    o_ref[...] = merged.astype(BF16)


def _merge(u, w_in, ya, yb, wa, wb, layer, *, tm=1024, tn=512):
    t = u.shape[0]
    ga0 = OFF_G_A // tn
    gb0 = OFF_G_B // tn
    return pl.pallas_call(
        _merge_kernel,
        grid=(t // tm, D_MODEL // tn),
        in_specs=[
            pl.BlockSpec((tm, D_MODEL), lambda i, j: (i, 0)),
            pl.BlockSpec((None, D_MODEL, tn), lambda i, j: (layer, 0, ga0 + j)),
            pl.BlockSpec((None, D_MODEL, tn), lambda i, j: (layer, 0, gb0 + j)),
            pl.BlockSpec((tm, SB_WIDTH), lambda i, j: (i, 0)),
            pl.BlockSpec((tm, DIFF_V_WIDTH), lambda i, j: (i, 0)),
            pl.BlockSpec((None, SB_WIDTH, tn), lambda i, j: (layer, 0, j)),
            pl.BlockSpec((None, DIFF_V_WIDTH, tn), lambda i, j: (layer, 0, j)),
        ],
        out_specs=pl.BlockSpec((tm, tn), lambda i, j: (i, j)),
        out_shape=jax.ShapeDtypeStruct((t, D_MODEL), BF16),
        compiler_params=_params("parallel", "arbitrary"),
        name="merge",
    )(u, w_in, w_in, ya, yb, wa, wb)


def _out_kernel(h_ref, m_ref, w_ref, o_ref):
    o_ref[...] = h_ref[...] + _dot(m_ref[...], w_ref[...])


def _out_proj(h, merged, w_out, layer, *, tm=1024, tn=1024):
    t = h.shape[0]
    return pl.pallas_call(
        _out_kernel,
        grid=(t // tm, D_MODEL // tn),
        in_specs=[
            pl.BlockSpec((tm, tn), lambda i, j: (i, j)),
            pl.BlockSpec((tm, D_MODEL), lambda i, j: (i, 0)),
            pl.BlockSpec((None, D_MODEL, tn), lambda i, j: (layer, 0, j)),
        ],
        out_specs=pl.BlockSpec((tm, tn), lambda i, j: (i, j)),
        out_shape=jax.ShapeDtypeStruct((t, D_MODEL), F32),
        compiler_params=_params("parallel", "arbitrary"),
        name="out_proj",
    )(h, merged, w_out)


def _ple_kernel(h_ref, g_ref, wg_ref, p_ref, wp_ref, og_ref, o_ref, *, segment_order_out):
    h = h_ref[...]
    u = _rms(h, g_ref[...]).astype(BF16)
    gate = jax.nn.sigmoid(_dot(u, wg_ref[...]))
    p = _flip_row_groups(p_ref[...]).astype(BF16)
    out = h + gate * _rms(_dot(p, wp_ref[...]), og_ref[...])
    o_ref[...] = _flip_row_groups(out) if segment_order_out else out


def _ple(h, gain, w_gate, p, w_proj, out_gain, layer, *, segment_order_out=False):
    t = h.shape[0]
    tm = ATTN_BLOCK
    nt = t // tm
    return pl.pallas_call(
        functools.partial(_ple_kernel, segment_order_out=segment_order_out),
        grid=(nt,),
        in_specs=[
            pl.BlockSpec((tm, D_MODEL), lambda i: (i, 0)),
            pl.BlockSpec((None, 1, D_MODEL), lambda i: (layer, 0, 0)),
            pl.BlockSpec((None, D_MODEL, D_MODEL), lambda i: (layer, 0, 0)),
            pl.BlockSpec((None, tm, PLE_DIM), lambda i: (layer, i, 0)),
            pl.BlockSpec((None, PLE_DIM, D_MODEL), lambda i: (layer, 0, 0)),
            pl.BlockSpec((None, 1, D_MODEL), lambda i: (layer, 0, 0)),
        ],
        out_specs=pl.BlockSpec((tm, D_MODEL), lambda i: (i, 0)),
        out_shape=jax.ShapeDtypeStruct((t, D_MODEL), F32),
        compiler_params=_params("parallel"),
        name="ple",
    )(h, gain, w_gate, p, w_proj, out_gain)


def _to_segment_order(a, axis, reverse_positions=False):
    seg = ATTN_BLOCK // SUBLANES
    shape = a.shape
    a = a.reshape(shape[:axis] + (shape[axis] // ATTN_BLOCK, SUBLANES, seg) + shape[axis + 1:])
    if reverse_positions:
        a = jnp.flip(a, axis + 2)
    return jnp.swapaxes(a, axis + 1, axis + 2).reshape(shape)


def _from_segment_order(a, axis):
    seg = ATTN_BLOCK // SUBLANES
    shape = a.shape
    a = a.reshape(shape[:axis] + (shape[axis] // ATTN_BLOCK, seg, SUBLANES) + shape[axis + 1:])
    return jnp.swapaxes(a, axis + 1, axis + 2).reshape(shape)


def _rope_tables(seq):
    pos = jnp.arange(seq, dtype=F32)
    inv_freq = ROPE_THETA ** (-jnp.arange(0, ROT_DIM, 2, dtype=F32) / ROT_DIM)
    ang = _to_segment_order(pos[:, None] * inv_freq[None, :], 0, reverse_positions=True)
    return jnp.cos(ang).T, jnp.sin(ang).T


def kernel(x, p, ffn1_norm, ffn1_w_gu, ffn1_w_down, mix_norm, w_in, diff_q_norm, diff_k_norm,
           diff_lambda_q1, diff_lambda_k1, diff_lambda_q2, diff_lambda_k2, diff_sub_norm,
           w_branch_a, w_branch_b, w_out, ffn2_norm, ffn2_w_gu, ffn2_w_down, ple_norm,
           ple_w_gate, ple_w_proj, ple_out_norm):
    batch, seq, _ = x.shape
    t = batch * seq
    row = lambda a: a.reshape(DEPTH, 1, a.shape[-1])
    bf = lambda a: a.astype(BF16)

    ffn_weights = [w for layer in range(DEPTH)
                   for w in ((ffn1_w_gu, ffn1_w_down, layer), (ffn2_w_gu, ffn2_w_down, layer))]
    ffn_bf16 = (bf(ffn1_w_gu[0]), bf(ffn1_w_down[0]))
    w_in, w_branch_a, w_branch_b, w_out = map(bf, (w_in, w_branch_a, w_branch_b, w_out))
    ple_w_gate, ple_w_proj = bf(ple_w_gate), bf(ple_w_proj)
    ffn1_norm, mix_norm, ffn2_norm, ple_norm, ple_out_norm, diff_sub_norm = map(
        row, (ffn1_norm, mix_norm, ffn2_norm, ple_norm, ple_out_norm, diff_sub_norm))
    lams = tuple(map(row, (diff_lambda_q1, diff_lambda_k1, diff_lambda_q2, diff_lambda_k2)))
    qk_gain = jnp.broadcast_to(jnp.stack([diff_q_norm, diff_k_norm])[..., None],
                               (2, DEPTH, HEAD_DIM, HEAD_DIM))
    cos_t, sin_t = _rope_tables(seq)
    p = _to_segment_order(p.reshape(DEPTH, t, PLE_DIM), 1)

    h = _to_segment_order(x.reshape(t, D_MODEL), 0)
    for layer in range(DEPTH):
        lambda_init = 0.8 - 0.6 * math.exp(-0.3 * layer)
        h, ffn_bf16 = _ffn(h, ffn1_norm, layer, *ffn_bf16, ffn_weights[2 * layer + 1],
                           segment_order_in=(layer == 0))

        qkv, u = _mixer_proj(h, mix_norm, w_in, qk_gain, cos_t, sin_t, layer, seq)
        y_a = _sb_attention(qkv, batch, seq)
        y_b = _diff_attention(lams, diff_sub_norm, qkv, layer, batch, seq, lambda_init)
        merged = _merge(u, w_in, y_a, y_b, w_branch_a, w_branch_b, layer)
        h = _out_proj(h, merged, w_out, layer)

        following = ffn_weights[2 * layer + 2] if layer + 1 < DEPTH else None
        h, ffn_bf16 = _ffn(h, ffn2_norm, layer, *ffn_bf16, following)
        h = _ple(h, ple_norm, ple_w_gate, p, ple_w_proj, ple_out_norm, layer,
                 segment_order_out=(layer == DEPTH - 1))
    return _from_segment_order(h, 0).reshape(batch, seq, D_MODEL)
```

```python
import functools
import math

import jax
import jax.numpy as jnp
from jax import lax
from jax.experimental import pallas as pl
from jax.experimental.pallas import tpu as pltpu

D_MODEL = 2048
DEPTH = 2
HEAD_DIM = 128
SB_HEADS = 8
DIFF_HEADS = 4
DIFF_V_DIM = 2 * HEAD_DIM
SB_WIDTH = SB_HEADS * HEAD_DIM
DIFF_QK_WIDTH = DIFF_HEADS * 2 * HEAD_DIM
DIFF_V_WIDTH = DIFF_HEADS * DIFF_V_DIM
D_FF = 5632
PLE_DIM = 256
ROPE_THETA = 500000.0
ROT_DIM = HEAD_DIM // 4
EPS = 1e-6
FFN_RES_WEIGHT = 0.5
QK_SCALE = HEAD_DIM ** -0.5 * math.log2(math.e)
SB_Q_SCALE = -0.5 * HEAD_DIM ** -0.5

OFF_DF_Q = 3 * SB_WIDTH
OFF_DF_V = OFF_DF_Q + 2 * DIFF_QK_WIDTH
OFF_G_A = OFF_DF_V + DIFF_V_WIDTH
OFF_G_B = OFF_G_A + D_MODEL

F32 = jnp.float32
BF16 = jnp.bfloat16

VMEM_LIMIT_BYTES = 60 * 1024 * 1024
SUBLANES = 8
MXU_DIM = 256
ATTN_BLOCK = 512
MASK_VALUE = -1e30

MIX_BLOCK = 1024
SB_Q_BLOCK, SB_K_BLOCK, SB_V_BLOCK, DF_Q_BLOCK, DF_K_BLOCK, DF_V_BLOCK = range(6)
HEADS_PER_MIX_BLOCK = MIX_BLOCK // HEAD_DIM
SB_HEADS_PER_STEP = 4
DIFF_HEADS_PER_STEP = 2


def _params(*sem):
    return pltpu.CompilerParams(dimension_semantics=sem, vmem_limit_bytes=VMEM_LIMIT_BYTES)


def _rms(x, g):
    ms = jnp.mean(x * x, axis=-1, keepdims=True)
    return x * lax.rsqrt(ms + EPS) * g


def _dot(a, b):
    return jnp.dot(a, b, preferred_element_type=F32)


def _dot_nt(a, b):
    return lax.dot_general(a, b, (((1,), (1,)), ((), ())), preferred_element_type=F32)


def _dot_tn_nt(a, b):
    return lax.dot_general(a, b, (((0,), (1,)), ((), ())), preferred_element_type=F32)


def _dot_tn(a, b):
    return lax.dot_general(a, b, (((0,), (0,)), ((), ())), preferred_element_type=F32)


def _flip_row_groups(x):
    n = x.shape[0] // SUBLANES
    return jnp.concatenate(
        [x[(n - 1 - g) * SUBLANES:(n - g) * SUBLANES] for g in range(n)], axis=0)


def _block_position(idx):
    seg = ATTN_BLOCK // SUBLANES
    return (idx % SUBLANES) * seg + (seg - 1 - idx // SUBLANES)


def _ffn_kernel(*refs, tn, segment_order_in, n_casts):
    h_ref, g_ref, wg_ref, wu_ref, wd_ref = refs[:5]
    f32_slabs = refs[5:5 + n_casts]
    o_ref = refs[5 + n_casts]
    bf16_slabs = refs[6 + n_casts:6 + 2 * n_casts]
    u_ref = refs[-1]
    for src, dst in zip(f32_slabs, bf16_slabs):
        dst[...] = src[...].astype(BF16)

    @pl.when(pl.program_id(1) == 0)
    def _():
        h = h_ref[...]
        if segment_order_in:
            h = jnp.concatenate([_flip_row_groups(h[b:b + ATTN_BLOCK])
                                 for b in range(0, h.shape[0], ATTN_BLOCK)], axis=0)
        u_ref[...] = _rms(h, g_ref[...]).astype(BF16)
        o_ref[...] = h

    u = u_ref[...]
    acts = []
    for c in range(0, wg_ref.shape[1], MXU_DIM):
        sub = slice(c, c + MXU_DIM)
        gate = _dot(u, wg_ref[:, sub])
        up = _dot(u, wu_ref[:, sub])
        acts.append((sub, (gate * jax.nn.sigmoid(gate) * up).astype(BF16)))
    for sub, a in acts:
        for n in range(D_MODEL // tn):
            cols = slice(n * tn, (n + 1) * tn)
            o_ref[:, cols] += FFN_RES_WEIGHT * _dot(a, wd_ref[sub, cols])


def _ffn(h, gain, layer, w_gu, w_down, next_weights=None, also_cast=None, *,
         segment_order_in=False, tm=1024, tf=512, tn=512):
    t = h.shape[0]
    assert tm % ATTN_BLOCK == 0
    nt, nf = t // tm, D_FF // tf
    in_specs = [
        pl.BlockSpec((tm, D_MODEL), lambda i, k: (i, 0)),
        pl.BlockSpec((None, 1, D_MODEL), lambda i, k: (layer, 0, 0)),
        pl.BlockSpec((D_MODEL, tf), lambda i, k: (0, k)),
        pl.BlockSpec((D_MODEL, tf), lambda i, k: (0, k + nf)),
        pl.BlockSpec((tf, D_MODEL), lambda i, k: (k, 0)),
    ]
    out_specs = [pl.BlockSpec((tm, D_MODEL), lambda i, k: (i, 0))]
    out_shape = [jax.ShapeDtypeStruct((t, D_MODEL), F32)]
    args = [h, gain, w_gu, w_gu, w_down]
    if next_weights is not None:
        next_gu, next_down, next_layer = next_weights
        gu_slab, down_slab = (D_MODEL // nt, 2 * D_FF // nf), (D_FF // nf, D_MODEL // nt)
        in_specs += [pl.BlockSpec((None,) + gu_slab, lambda i, k: (next_layer, i, k)),
                     pl.BlockSpec((None,) + down_slab, lambda i, k: (next_layer, k, i))]
        out_specs += [pl.BlockSpec(gu_slab, lambda i, k: (i, k)),
                      pl.BlockSpec(down_slab, lambda i, k: (k, i))]
        out_shape += [jax.ShapeDtypeStruct((D_MODEL, 2 * D_FF), BF16),
                      jax.ShapeDtypeStruct((D_FF, D_MODEL), BF16)]
        args += [next_gu, next_down]
    if also_cast is not None:
        stacked, src_layer = also_cast
        last = stacked.shape[2] // MIX_BLOCK - 1
        slab = (stacked.shape[1] // nt, MIX_BLOCK)
        in_specs.append(pl.BlockSpec((None,) + slab,
                                     lambda i, k: (src_layer, i, jnp.minimum(k, last))))
        out_specs.append(pl.BlockSpec(slab, lambda i, k: (i, jnp.minimum(k, last))))
        out_shape.append(jax.ShapeDtypeStruct(stacked.shape[1:], BF16))
        args.append(stacked)
    out = pl.pallas_call(
        functools.partial(_ffn_kernel, tn=tn, segment_order_in=segment_order_in,
                          n_casts=len(args) - 5),
        grid=(nt, nf),
        in_specs=in_specs,
        out_specs=out_specs,
        out_shape=out_shape,
        scratch_shapes=[pltpu.VMEM((tm, D_MODEL), BF16)],
        compiler_params=_params("parallel", "arbitrary"),
        name="ffn",
    )(*args)
    return out[0], tuple(out[1:])


def _mixer_proj_kernel(h_ref, g_ref, w_ref, qkg_ref, cos_ref, sin_ref, o_ref, u_ref, *, tm):
    j = pl.program_id(1)

    @pl.when(j == 0)
    def _():
        u_ref[...] = _rms(h_ref[...], g_ref[...]).astype(BF16)

    u = u_ref[...]

    def store(head, y):
        for b in range(tm // ATTN_BLOCK):
            o_ref[head, b] = y[:, b * ATTN_BLOCK:(b + 1) * ATTN_BLOCK].astype(BF16)

    def run(finish):
        starts = list(range(0, MIX_BLOCK, MXU_DIM))
        r = _dot_tn_nt(w_ref[:, starts[0]:starts[0] + MXU_DIM], u)
        for prev, r0 in zip(starts, starts[1:]):
            r_next = _dot_tn_nt(w_ref[:, r0:r0 + MXU_DIM], u)
            finish(r, prev)
            r = r_next
        finish(r, starts[-1])

    def heads_of(r, r0):
        for c in range(MXU_DIM // HEAD_DIM):
            yield r0 // HEAD_DIM + c, r[c * HEAD_DIM:(c + 1) * HEAD_DIM, :]

    is_rotary = jnp.logical_or(j == DF_Q_BLOCK, j == DF_K_BLOCK)

    @pl.when(jnp.logical_not(is_rotary))
    def _():
        scale = jnp.where(j == SB_Q_BLOCK, SB_Q_SCALE, 1.0).astype(F32)

        def finish(r, r0):
            for head, x in heads_of(r, r0):
                store(head, x * scale)

        run(finish)

    @pl.when(is_rotary)
    def _():
        gain = jnp.concatenate([qkg_ref[...]] * (tm // HEAD_DIM), axis=1)
        gain = gain * jnp.where(j == DF_Q_BLOCK, QK_SCALE, 1.0).astype(F32)
        cos = cos_ref[...]
        sin = sin_ref[...]
        half = ROT_DIM // 2

        def finish(r, r0):
            for head, x in heads_of(r, r0):
                ms = jnp.mean(x * x, axis=0, keepdims=True)
                y = x * lax.rsqrt(ms + EPS) * gain
                y1, y2 = y[:half], y[half:ROT_DIM]
                store(head, jnp.concatenate(
                    [y1 * cos - y2 * sin, y2 * cos + y1 * sin, y[ROT_DIM:]], axis=0))

        run(finish)


def _mixer_proj(h, gain, w_in, qk_gain, cos_t, sin_t, layer, seq, *, tm=1024):
    t = h.shape[0]
    nblocks = OFF_G_A // MIX_BLOCK
    nseq = seq // tm
    return pl.pallas_call(
        functools.partial(_mixer_proj_kernel, tm=tm),
        grid=(t // tm, nblocks),
        in_specs=[
            pl.BlockSpec((tm, D_MODEL), lambda i, j: (i, 0)),
            pl.BlockSpec((None, 1, D_MODEL), lambda i, j: (layer, 0, 0)),
            pl.BlockSpec((None, D_MODEL, MIX_BLOCK), lambda i, j: (0, 0, j)),
            pl.BlockSpec((None, None, HEAD_DIM, HEAD_DIM),
                         lambda i, j: (jnp.clip(j - DF_Q_BLOCK, 0, 1), layer, 0, 0)),
            pl.BlockSpec((ROT_DIM // 2, tm), lambda i, j: (0, i % nseq)),
            pl.BlockSpec((ROT_DIM // 2, tm), lambda i, j: (0, i % nseq)),
        ],
        out_specs=[pl.BlockSpec((HEADS_PER_MIX_BLOCK, tm // ATTN_BLOCK, HEAD_DIM, ATTN_BLOCK),
                                lambda i, j: (j, i, 0, 0)),
                   pl.BlockSpec((tm, D_MODEL), lambda i, j: (i, 0))],
        out_shape=[jax.ShapeDtypeStruct(
            (nblocks * HEADS_PER_MIX_BLOCK, t // ATTN_BLOCK, HEAD_DIM, ATTN_BLOCK), BF16),
            jax.ShapeDtypeStruct((t, D_MODEL), BF16)],
        compiler_params=_params("parallel", "arbitrary"),
        name="mixer_proj",
    )(h, gain, w_in, qk_gain, cos_t, sin_t)


def _sb_kernel(q_ref, k_ref, v_ref, o_ref, z_scr, w_scr, acc_scr):
    tq = tk = ATTN_BLOCK
    heads = range(SB_HEADS_PER_STEP)
    ngroups = tk // SUBLANES
    seg = tk // SUBLANES
    group = lambda g: slice(g * SUBLANES, (g + 1) * SUBLANES)

    def logits(qi, kb):
        for hd in heads:
            z_scr[hd] = _dot_tn(k_ref[hd, kb], q_ref[hd, qi])

    def weights(masked):
        if masked:
            sub = lax.broadcasted_iota(jnp.int32, (SUBLANES, tq), 0)
            t_pos = _block_position(lax.broadcasted_iota(jnp.int32, (SUBLANES, tq), 1))
        run = [jnp.ones((SUBLANES, tq), F32) for _ in heads]
        for g in range(ngroups):
            for hd in heads:
                th = 0.5 * jnp.tanh(z_scr[hd, group(g), :])
                if masked:
                    th = jnp.where(sub * seg + (seg - 1 - g) < t_pos, th, 0.5)
                w_scr[hd, group(g), :] = (0.5 - th) * run[hd]
                run[hd] = run[hd] * (0.5 + th)
        return tuple(run)

    def accumulate(kb, totals, carries):
        out = []
        for hd in heads:
            carry = carries[hd]
            offs = [None] * SUBLANES
            for i in reversed(range(SUBLANES)):
                offs[i] = carry
                carry = carry * totals[hd][i:i + 1, :]
            off = jnp.concatenate(offs, axis=0)
            attn = jnp.concatenate([w_scr[hd, group(g), :] * off for g in range(ngroups)],
                                   axis=0).astype(BF16)
            acc_scr[hd] += _dot(v_ref[hd, kb], attn)
            out.append(carry)
        return tuple(out)

    def query_block(qi, _):
        def body(it, state):
            totals, carries = state
            kb = qi - it
            carries = accumulate(kb, totals, carries)
            totals = weights(False)
            logits(qi, jnp.maximum(kb - 2, 0))
            return totals, carries

        acc_scr[...] = jnp.zeros_like(acc_scr)
        logits(qi, qi)
        totals = weights(True)
        logits(qi, jnp.maximum(qi - 1, 0))
        ones = tuple(jnp.ones((1, tq), F32) for _ in heads)
        totals, carries = lax.fori_loop(0, qi, body, (totals, ones))
        accumulate(0, totals, carries)
        for hd in heads:
            o_ref[pl.ds(pl.multiple_of(qi * tq, tq), tq), hd * HEAD_DIM:(hd + 1) * HEAD_DIM] = (
                acc_scr[hd].T.astype(BF16))
        return 0

    lax.fori_loop(0, q_ref.shape[1], query_block, 0)


def _sb_attention(qkv, batch, seq):
    nq = seq // ATTN_BLOCK
    t = batch * seq
    hps = SB_HEADS_PER_STEP
    head_seq = (hps, nq, HEAD_DIM, ATTN_BLOCK)
    first = lambda block: block * HEADS_PER_MIX_BLOCK // hps
    return pl.pallas_call(
        _sb_kernel,
        grid=(batch, SB_HEADS // hps),
        in_specs=[
            pl.BlockSpec(head_seq, lambda b, h: (first(SB_Q_BLOCK) + h, b, 0, 0)),
            pl.BlockSpec(head_seq, lambda b, h: (first(SB_K_BLOCK) + h, b, 0, 0)),
            pl.BlockSpec(head_seq, lambda b, h: (first(SB_V_BLOCK) + h, b, 0, 0)),
        ],
        out_specs=pl.BlockSpec((seq, hps * HEAD_DIM), lambda b, h: (b, h)),
        out_shape=jax.ShapeDtypeStruct((t, SB_WIDTH), BF16),
        scratch_shapes=[pltpu.VMEM((hps, ATTN_BLOCK, ATTN_BLOCK), F32),
                        pltpu.VMEM((hps, ATTN_BLOCK, ATTN_BLOCK), F32),
                        pltpu.VMEM((hps, HEAD_DIM, ATTN_BLOCK), F32)],
        compiler_params=_params("parallel", "parallel"),
        name="sb_attention",
    )(qkv, qkv, qkv)


def _diff_kernel(lq1_ref, lk1_ref, lq2_ref, lk2_ref, sub_ref, q_ref, k_ref, v_ref, o_ref,
                 s_scr, acc_scr, *, lambda_init):
    tq = tk = ATTN_BLOCK
    chains = [(hd, half) for hd in range(DIFF_HEADS_PER_STEP) for half in range(2)]
    entry = lambda hd, half: 2 * hd + half

    def logits(qi, kb, masked):
        maxima = []
        for hd, half in chains:
            c = entry(hd, half)
            s = _dot_tn(k_ref[c, kb], q_ref[c, qi])
            if masked:
                s_pos = _block_position(lax.broadcasted_iota(jnp.int32, (tk, tq), 0))
                t_pos = _block_position(lax.broadcasted_iota(jnp.int32, (tk, tq), 1))
                s = jnp.where(s_pos <= t_pos, s, MASK_VALUE)
            s_scr[c] = s
            maxima.append(jnp.max(s, axis=0, keepdims=True))
        return tuple(maxima)

    def accumulate(kb, maxima, stats):
        new_stats = []
        for hd, half in chains:
            c = entry(hd, half)
            v_t = jnp.concatenate([v_ref[entry(hd, 0), kb], v_ref[entry(hd, 1), kb]], axis=0)
            m, l = stats[c]
            m_new = jnp.maximum(m, maxima[c])
            alpha = jnp.exp2(m - m_new)
            p = jnp.exp2(s_scr[c] - m_new)
            l = alpha * l + jnp.sum(p, axis=0, keepdims=True)
            acc_scr[c] = alpha * acc_scr[c] + _dot(v_t, p.astype(BF16))
            new_stats.append((m_new, l))
        return tuple(new_stats)

    lam = (jnp.exp(jnp.sum(lq1_ref[...] * lk1_ref[...], axis=1, keepdims=True))
           - jnp.exp(jnp.sum(lq2_ref[...] * lk2_ref[...], axis=1, keepdims=True))
           + lambda_init)
    nq = q_ref.shape[1]

    def query_block(qi, maxima):
        def body(it, state):
            maxima, stats = state
            kb = qi - it
            stats = accumulate(kb, maxima, stats)
            return logits(qi, kb - 1, False), stats

        init = (jnp.full((1, tq), MASK_VALUE, F32), jnp.zeros((1, tq), F32))
        maxima, stats = lax.fori_loop(0, qi, body, (maxima, (init,) * len(chains)))
        stats = accumulate(0, maxima, stats)
        for hd in range(DIFF_HEADS_PER_STEP):
            (_, l1), (_, l2) = stats[entry(hd, 0)], stats[entry(hd, 1)]
            y = (acc_scr[entry(hd, 0)] / l1 - lam * (acc_scr[entry(hd, 1)] / l2)).T
            o_ref[pl.ds(pl.multiple_of(qi * tq, tq), tq), hd * DIFF_V_DIM:(hd + 1) * DIFF_V_DIM] = (
                _rms(y, sub_ref[...]) * (1.0 - lambda_init)).astype(BF16)
        acc_scr[...] = jnp.zeros_like(acc_scr)
        nxt = jnp.minimum(qi + 1, nq - 1)
        return logits(nxt, nxt, True)

    acc_scr[...] = jnp.zeros_like(acc_scr)
    lax.fori_loop(0, nq, query_block, logits(0, 0, True))


def _diff_attention(lams, sub_gain, qkv, layer, batch, seq, lambda_init):
    nq = seq // ATTN_BLOCK
    t = batch * seq
    hps = DIFF_HEADS_PER_STEP
    head_seq = (2 * hps, nq, HEAD_DIM, ATTN_BLOCK)
    first = lambda block: block * HEADS_PER_MIX_BLOCK // (2 * hps)
    lam_spec = pl.BlockSpec((None, 1, HEAD_DIM), lambda b, h: (layer, 0, 0))
    return pl.pallas_call(
        functools.partial(_diff_kernel, lambda_init=lambda_init),
        grid=(batch, DIFF_HEADS // hps),
        in_specs=[
            lam_spec, lam_spec, lam_spec, lam_spec,
            pl.BlockSpec((None, 1, DIFF_V_DIM), lambda b, h: (layer, 0, 0)),
            pl.BlockSpec(head_seq, lambda b, h: (first(DF_Q_BLOCK) + h, b, 0, 0)),
            pl.BlockSpec(head_seq, lambda b, h: (first(DF_K_BLOCK) + h, b, 0, 0)),
            pl.BlockSpec(head_seq, lambda b, h: (first(DF_V_BLOCK) + h, b, 0, 0)),
        ],
        out_specs=pl.BlockSpec((seq, hps * DIFF_V_DIM), lambda b, h: (b, h)),
        out_shape=jax.ShapeDtypeStruct((t, DIFF_V_WIDTH), BF16),
        scratch_shapes=[pltpu.VMEM((2 * hps, ATTN_BLOCK, ATTN_BLOCK), F32),
                        pltpu.VMEM((2 * hps, DIFF_V_DIM, ATTN_BLOCK), F32)],
        compiler_params=_params("parallel", "parallel"),
        name="diff_attention",
    )(*lams, sub_gain, qkv, qkv, qkv)


def _merge_kernel(u_ref, wga_ref, wgb_ref, ya_ref, yb_ref, wa_ref, wb_ref, o_ref):
    u = u_ref[...]
    merged = (jax.nn.sigmoid(_dot(u, wga_ref[...])) * _dot(ya_ref[...], wa_ref[...])
              + jax.nn.sigmoid(_dot(u, wgb_ref[...])) * _dot(yb_ref[...], wb_ref[...]))
    o_ref[...] = merged.astype(BF16)


def _merge(u, w_in, ya, yb, wa, wb, layer, *, tm=1024, tn=512):
    t = u.shape[0]
    ga0 = OFF_G_A // tn
    gb0 = OFF_G_B // tn
    return pl.pallas_call(
        _merge_kernel,
        grid=(t // tm, D_MODEL // tn),
        in_specs=[
            pl.BlockSpec((tm, D_MODEL), lambda i, j: (i, 0)),
            pl.BlockSpec((None, D_MODEL, tn), lambda i, j: (0, 0, ga0 + j)),
            pl.BlockSpec((None, D_MODEL, tn), lambda i, j: (0, 0, gb0 + j)),
            pl.BlockSpec((tm, SB_WIDTH), lambda i, j: (i, 0)),
            pl.BlockSpec((tm, DIFF_V_WIDTH), lambda i, j: (i, 0)),
            pl.BlockSpec((None, SB_WIDTH, tn), lambda i, j: (layer, 0, j)),
            pl.BlockSpec((None, DIFF_V_WIDTH, tn), lambda i, j: (layer, 0, j)),
        ],
        out_specs=pl.BlockSpec((tm, tn), lambda i, j: (i, j)),
        out_shape=jax.ShapeDtypeStruct((t, D_MODEL), BF16),
        compiler_params=_params("parallel", "arbitrary"),
        name="merge",
    )(u, w_in, w_in, ya, yb, wa, wb)


def _out_kernel(h_ref, m_ref, w_ref, o_ref):
    o_ref[...] = h_ref[...] + _dot(m_ref[...], w_ref[...])


def _out_proj(h, merged, w_out, layer, *, tm=1024, tn=1024):
    t = h.shape[0]
    return pl.pallas_call(
        _out_kernel,
        grid=(t // tm, D_MODEL // tn),
        in_specs=[
            pl.BlockSpec((tm, tn), lambda i, j: (i, j)),
            pl.BlockSpec((tm, D_MODEL), lambda i, j: (i, 0)),
            pl.BlockSpec((None, D_MODEL, tn), lambda i, j: (layer, 0, j)),
        ],
        out_specs=pl.BlockSpec((tm, tn), lambda i, j: (i, j)),
        out_shape=jax.ShapeDtypeStruct((t, D_MODEL), F32),
        compiler_params=_params("parallel", "arbitrary"),
        name="out_proj",
    )(h, merged, w_out)


def _ple_kernel(h_ref, g_ref, wg_ref, p_ref, wp_ref, og_ref, o_ref, *, segment_order_out):
    h = h_ref[...]
    u = _rms(h, g_ref[...]).astype(BF16)
    gate = jax.nn.sigmoid(_dot(u, wg_ref[...]))
    p = _flip_row_groups(p_ref[...]).astype(BF16)
    out = h + gate * _rms(_dot(p, wp_ref[...]), og_ref[...])
    o_ref[...] = _flip_row_groups(out) if segment_order_out else out


def _ple(h, gain, w_gate, p, w_proj, out_gain, layer, *, segment_order_out=False):
    t = h.shape[0]
    tm = ATTN_BLOCK
    nt = t // tm
    return pl.pallas_call(
        functools.partial(_ple_kernel, segment_order_out=segment_order_out),
        grid=(nt,),
        in_specs=[
            pl.BlockSpec((tm, D_MODEL), lambda i: (i, 0)),
            pl.BlockSpec((None, 1, D_MODEL), lambda i: (layer, 0, 0)),
            pl.BlockSpec((None, D_MODEL, D_MODEL), lambda i: (layer, 0, 0)),
            pl.BlockSpec((None, tm, PLE_DIM), lambda i: (layer, i, 0)),
            pl.BlockSpec((None, PLE_DIM, D_MODEL), lambda i: (layer, 0, 0)),
            pl.BlockSpec((None, 1, D_MODEL), lambda i: (layer, 0, 0)),
        ],
        out_specs=pl.BlockSpec((tm, D_MODEL), lambda i: (i, 0)),
        out_shape=jax.ShapeDtypeStruct((t, D_MODEL), F32),
        compiler_params=_params("parallel"),
        name="ple",
    )(h, gain, w_gate, p, w_proj, out_gain)


def _to_segment_order(a, axis, reverse_positions=False):
    seg = ATTN_BLOCK // SUBLANES
    shape = a.shape
    a = a.reshape(shape[:axis] + (shape[axis] // ATTN_BLOCK, SUBLANES, seg) + shape[axis + 1:])
    if reverse_positions:
        a = jnp.flip(a, axis + 2)
    return jnp.swapaxes(a, axis + 1, axis + 2).reshape(shape)


def _from_segment_order(a, axis):
    seg = ATTN_BLOCK // SUBLANES
    shape = a.shape
    a = a.reshape(shape[:axis] + (shape[axis] // ATTN_BLOCK, seg, SUBLANES) + shape[axis + 1:])
    return jnp.swapaxes(a, axis + 1, axis + 2).reshape(shape)


def _rope_tables(seq):
    pos = jnp.arange(seq, dtype=F32)
    inv_freq = ROPE_THETA ** (-jnp.arange(0, ROT_DIM, 2, dtype=F32) / ROT_DIM)
    ang = _to_segment_order(pos[:, None] * inv_freq[None, :], 0, reverse_positions=True)
    return jnp.cos(ang).T, jnp.sin(ang).T


def kernel(x, p, ffn1_norm, ffn1_w_gu, ffn1_w_down, mix_norm, w_in, diff_q_norm, diff_k_norm,
           diff_lambda_q1, diff_lambda_k1, diff_lambda_q2, diff_lambda_k2, diff_sub_norm,
           w_branch_a, w_branch_b, w_out, ffn2_norm, ffn2_w_gu, ffn2_w_down, ple_norm,
           ple_w_gate, ple_w_proj, ple_out_norm):
    batch, seq, _ = x.shape
    t = batch * seq
    row = lambda a: a.reshape(DEPTH, 1, a.shape[-1])
    bf = lambda a: a.astype(BF16)

    ffn_weights = [w for layer in range(DEPTH)
                   for w in ((ffn1_w_gu, ffn1_w_down, layer), (ffn2_w_gu, ffn2_w_down, layer))]
    ffn_bf16 = (bf(ffn1_w_gu[0]), bf(ffn1_w_down[0]))
    w_branch_a, w_branch_b, w_out = map(bf, (w_branch_a, w_branch_b, w_out))
    ple_w_gate, ple_w_proj = bf(ple_w_gate), bf(ple_w_proj)
    ffn1_norm, mix_norm, ffn2_norm, ple_norm, ple_out_norm, diff_sub_norm = map(
        row, (ffn1_norm, mix_norm, ffn2_norm, ple_norm, ple_out_norm, diff_sub_norm))
    lams = tuple(map(row, (diff_lambda_q1, diff_lambda_k1, diff_lambda_q2, diff_lambda_k2)))
    qk_gain = jnp.broadcast_to(jnp.stack([diff_q_norm, diff_k_norm])[..., None],
                               (2, DEPTH, HEAD_DIM, HEAD_DIM))
    cos_t, sin_t = _rope_tables(seq)
    p = _to_segment_order(p.reshape(DEPTH, t, PLE_DIM), 1)

    h = _to_segment_order(x.reshape(t, D_MODEL), 0)
    for layer in range(DEPTH):
        lambda_init = 0.8 - 0.6 * math.exp(-0.3 * layer)
        h, (*ffn_bf16, w_in_l) = _ffn(h, ffn1_norm, layer, *ffn_bf16, ffn_weights[2 * layer + 1],
                                      (w_in, layer), segment_order_in=(layer == 0))
        w_in_l = w_in_l[None]

        qkv, u = _mixer_proj(h, mix_norm, w_in_l, qk_gain, cos_t, sin_t, layer, seq)
        y_a = _sb_attention(qkv, batch, seq)
        y_b = _diff_attention(lams, diff_sub_norm, qkv, layer, batch, seq, lambda_init)
        merged = _merge(u, w_in_l, y_a, y_b, w_branch_a, w_branch_b, layer)
        h = _out_proj(h, merged, w_out, layer)

        following = ffn_weights[2 * layer + 2] if layer + 1 < DEPTH else None
        h, ffn_bf16 = _ffn(h, ffn2_norm, layer, *ffn_bf16, following)
        h = _ple(h, ple_norm, ple_w_gate, p, ple_w_proj, ple_out_norm, layer,
                 segment_order_out=(layer == DEPTH - 1))
    return _from_segment_order(h, 0).reshape(batch, seq, D_MODEL)
```
